```python
import jax, jax.numpy as jnp
from jax import lax
import numpy as np

D_MODEL = 2048
BATCH = 2
SEQ = 8192
DEPTH = 2

HEAD_DIM = 64
N_HEADS_TOTAL = D_MODEL // HEAD_DIM
HA = N_HEADS_TOTAL // 4
KVA = 2
HB = N_HEADS_TOTAL // 4
KVB = 2
HC = N_HEADS_TOTAL - HA - HB
SCALE = HEAD_DIM ** -0.5
BLOCK = 128
WINDOW_A = 128
CMP_BLOCK = 32
CMP_STRIDE = 16
CMP_HIDDEN = 128
SEL_BLOCK = 64
N_SELECT = 16
WINDOW_B = 512
FORCE_SCORE = 1e4
N_GROUPS = 4
EXPERTS_PER_GROUP = 8
N_EXPERTS = N_GROUPS * EXPERTS_PER_GROUP
TOP_K = 2
D_EXPERT = 512
MOE_BLOCK = 128
IN_SIZES = (HA * HEAD_DIM, KVA * HEAD_DIM, KVA * HEAD_DIM,
            HB * HEAD_DIM,
            KVB * HEAD_DIM, KVB * HEAD_DIM,
            KVB * HEAD_DIM, KVB * HEAD_DIM,
            KVB * HEAD_DIM, KVB * HEAD_DIM,
            HB * 3,
            HC * HEAD_DIM, HC * HEAD_DIM, HC * HEAD_DIM)
D_IN = sum(IN_SIZES)

kernel_name = "hybrid_swa_nsa_stickbreak_hmoe_deepnorm"


def layer_norm(x, g=None, b=None, eps=1e-5):
    xf = x.astype(jnp.float32)
    mu = xf.mean(-1, keepdims=True)
    var = jnp.square(xf - mu).mean(-1, keepdims=True)
    y = (xf - mu) * lax.rsqrt(var + eps)
    if g is not None:
        y = y * g.astype(jnp.float32) + b.astype(jnp.float32)
    return y.astype(x.dtype)


def group_rms(o, eps=1e-6):
    B, S = o.shape[:2]
    of = o.reshape(B, S, -1).astype(jnp.float32)
    return (of * lax.rsqrt(jnp.mean(of * of, -1, keepdims=True) + eps)).astype(o.dtype)


def alibi_slopes(n):
    return jnp.exp2(-8.0 * jnp.arange(1, n + 1, dtype=jnp.float32) / n)


def masked_softmax(s, mask, sink=None):
    s = jnp.where(mask, s, -jnp.inf)
    m = jnp.max(s, -1, keepdims=True)
    if sink is not None:
        m = jnp.maximum(m, sink)
    m = jnp.where(jnp.isfinite(m), m, 0.0)
    e = jnp.exp(s - m)
    den = e.sum(-1, keepdims=True)
    if sink is not None:
        den = den + jnp.exp(sink - m)
    return e / jnp.maximum(den, 1e-30)


def banded_window_attention(q, k, v, window, slopes, sinks=None):
    B, S, H, D = q.shape
    G = k.shape[2]
    R = H // G
    nb = S // BLOCK
    span = window + BLOCK
    kp = jnp.pad(k, ((0, 0), (window, 0), (0, 0), (0, 0)))
    vp = jnp.pad(v, ((0, 0), (window, 0), (0, 0), (0, 0)))
    qb = q.reshape(B, nb, BLOCK, G, R, D)
    slope = slopes.reshape(G, R)[None, :, :, None, None]
    sink = None if sinks is None else sinks.astype(jnp.float32).reshape(G, R)[None, :, :, None, None]
    offs_q = jnp.arange(BLOCK)
    offs_k = jnp.arange(span)

    def one_block(i):
        qi = lax.dynamic_index_in_dim(qb, i, axis=1, keepdims=False)
        ki = lax.dynamic_slice_in_dim(kp, i * BLOCK, span, axis=1)
        vi = lax.dynamic_slice_in_dim(vp, i * BLOCK, span, axis=1)
        t = i * BLOCK + offs_q
        s_pos = i * BLOCK - window + offs_k
        dist = t[:, None] - s_pos[None, :]
        mask = (dist >= 0) & (dist < window) & (s_pos[None, :] >= 0)
        s = jnp.einsum('bqgrd,bkgd->bgrqk', qi, ki).astype(jnp.float32) * SCALE
        s = s - slope * dist.astype(jnp.float32)
        p = masked_softmax(s, mask, sink)
        o = jnp.einsum('bgrqk,bkgd->bqgrd', p.astype(vi.dtype), vi)
        return o.reshape(B, BLOCK, H, D)

    out = lax.map(one_block, jnp.arange(nb))
    return out.transpose(1, 0, 2, 3, 4).reshape(B, S, H, D)


def compress_blocks(x, pe, w1, w2):
    B, S, G, D = x.shape
    nc = (S - CMP_BLOCK) // CMP_STRIDE + 1
    idx = jnp.arange(nc)[:, None] * CMP_STRIDE + jnp.arange(CMP_BLOCK)[None, :]
    blocks = x[:, idx] + pe[None, None, :, None, :]
    flat = blocks.transpose(0, 1, 3, 2, 4).reshape(B, nc, G, CMP_BLOCK * D)
    return jax.nn.gelu(flat @ w1) @ w2


def nsa_compressed_and_selected(q, kc, vc, ks_all, vs_all, slopes):
    B, S, H, D = q.shape
    G = kc.shape[2]
    R = H // G
    nb = S // BLOCK
    nc = kc.shape[1]
    nsel = S // SEL_BLOCK
    n_pick = min(N_SELECT, nsel)
    qb = q.reshape(B, nb, BLOCK, G, R, D)
    cmp_start = jnp.arange(nc) * CMP_STRIDE
    cmp_end = cmp_start + CMP_BLOCK - 1
    sel_idx = jnp.arange(nsel)
    sel_start = sel_idx * SEL_BLOCK
    overlap = ((cmp_end[:, None] >= sel_start[None, :]) &
               (cmp_start[:, None] < sel_start[None, :] + SEL_BLOCK)).astype(jnp.float32)
    kst = ks_all.transpose(0, 2, 1, 3)
    vst = vs_all.transpose(0, 2, 1, 3)
    slope = slopes.reshape(G, R)[None, :, :, None, None]
    bi = jnp.arange(B)[:, None, None, None]
    gi = jnp.arange(G)[None, None, :, None]
    offs_q = jnp.arange(BLOCK)
    offs_sel = jnp.arange(SEL_BLOCK)

    def one_block(i):
        qi = lax.dynamic_index_in_dim(qb, i, axis=1, keepdims=False)
        t = i * BLOCK + offs_q
        dist_c = t[:, None] - cmp_end[None, :]
        sc = jnp.einsum('bqgrd,bngd->bgrqn', qi, kc).astype(jnp.float32) * SCALE
        sc = sc - slope * dist_c.astype(jnp.float32)
        pc = masked_softmax(sc, dist_c >= 0)
        o_c = jnp.einsum('bgrqn,bngd->bqgrd', pc.astype(vc.dtype), vc)
        imp = jnp.einsum('bgrqn,nj->bqgj', pc, overlap)
        cur = t // SEL_BLOCK
        forced = (sel_idx[None, :] == 0) | (sel_idx[None, :] == cur[:, None]) | (sel_idx[None, :] == cur[:, None] - 1)
        valid = sel_start[None, :] <= t[:, None]
        imp = jnp.where(valid[None, :, None, :],
                        jnp.where(forced[None, :, None, :], FORCE_SCORE, imp), -1.0)
        _, sel = lax.top_k(imp, n_pick)
        tok = (sel[..., None] * SEL_BLOCK + offs_sel).reshape(B, BLOCK, G, n_pick * SEL_BLOCK)
        ks = kst[bi, gi, tok]
        vs = vst[bi, gi, tok]
        dist_s = (t[None, :, None, None] - tok).transpose(0, 2, 1, 3)[:, :, None]
        ss = jnp.einsum('bqgrd,bqgkd->bgrqk', qi, ks).astype(jnp.float32) * SCALE
        ss = ss - slope * dist_s.astype(jnp.float32)
        ps = masked_softmax(ss, dist_s >= 0)
        o_s = jnp.einsum('bgrqk,bqgkd->bqgrd', ps.astype(vs.dtype), vs)
        return o_c.reshape(B, BLOCK, H, D), o_s.reshape(B, BLOCK, H, D)

    o_c, o_s = lax.map(one_block, jnp.arange(nb))
    unblock = lambda o: o.transpose(1, 0, 2, 3, 4).reshape(B, S, H, D)
    return unblock(o_c), unblock(o_s)


def stick_breaking_attention(q, k, v):
    B, S, H, D = q.shape
    nb = S // BLOCK
    qb = q.reshape(B, nb, BLOCK, H, D)
    s_pos = jnp.arange(S)
    offs_q = jnp.arange(BLOCK)
    vf = v.astype(jnp.float32)

    def one_block(i):
        qi = lax.dynamic_index_in_dim(qb, i, axis=1, keepdims=False)
        t = i * BLOCK + offs_q
        mask = s_pos[None, :] < t[:, None]
        z = jnp.einsum('bqhd,bkhd->bhqk', qi, k).astype(jnp.float32) * SCALE
        log_1mb = jnp.where(mask, jax.nn.log_sigmoid(-z), 0.0)
        tail = lax.cumsum(log_1mb, axis=3, reverse=True) - log_1mb
        a = jnp.where(mask, jnp.exp(jax.nn.log_sigmoid(z) + tail), 0.0)
        return jnp.einsum('bhqk,bkhd->bqhd', a, vf)

    out = lax.map(one_block, jnp.arange(nb))
    return out.transpose(1, 0, 2, 3, 4).reshape(B, S, H, D).astype(q.dtype)


def hybrid_mixer(h, w_in, sinks, cmp_pe_k, cmp_w1_k, cmp_w2_k, cmp_pe_v, cmp_w1_v, cmp_w2_v, mix_gain, w_out):
    B, S, _ = h.shape
    proj = h @ w_in
    split_at = np.cumsum(IN_SIZES)[:-1].tolist()
    (qa, ka, va, qb, kb_c, vb_c, kb_s, vb_s, kb_w, vb_w, gb, qc, kc, vc) = jnp.split(proj, split_at, axis=-1)
    heads = lambda t: t.reshape(B, S, -1, HEAD_DIM)
    slopes = alibi_slopes(HA + HB)
    o_a = banded_window_attention(heads(qa), heads(ka), heads(va), WINDOW_A, slopes[:HA], sinks)
    qbh = heads(qb)
    k_cmp = compress_blocks(heads(kb_c), cmp_pe_k, cmp_w1_k, cmp_w2_k)
    v_cmp = compress_blocks(heads(vb_c), cmp_pe_v, cmp_w1_v, cmp_w2_v)
    o_cmp, o_slc = nsa_compressed_and_selected(qbh, k_cmp, v_cmp, heads(kb_s), heads(vb_s), slopes[HA:])
    o_win = banded_window_attention(qbh, heads(kb_w), heads(vb_w), WINDOW_B, slopes[HA:])
    g = jax.nn.sigmoid(gb.reshape(B, S, HB, 3))
    o_b = g[..., 0:1] * o_cmp + g[..., 1:2] * o_slc + g[..., 2:3] * o_win
    o_c = stick_breaking_attention(heads(qc), heads(kc), heads(vc))
    merged = jnp.concatenate([group_rms(o_a), group_rms(o_b), group_rms(o_c)], axis=-1) * mix_gain
    return merged @ w_out


def hierarchical_moe(h, w_rg, b_rg, w_re, b_re, w_gate, w_up, w_down):
    B, S, D = h.shape
    N = B * S
    xt = h.reshape(N, D)
    pg = jax.nn.softmax((xt @ w_rg + b_rg).astype(jnp.float32), axis=-1)
    g_sel = jnp.argmax(pg, axis=-1)
    p_group = jnp.max(pg, axis=-1)
    le = jnp.einsum('nd,gde->nge', xt, w_re) + b_re
    le_sel = jnp.take_along_axis(le, g_sel[:, None, None], axis=1)[:, 0]
    pe = jax.nn.softmax(le_sel.astype(jnp.float32), axis=-1)
    top_p, top_e = lax.top_k(pe, TOP_K)
    gate = p_group[:, None] * top_p / jnp.sum(top_p, -1, keepdims=True)
    expert = g_sel[:, None] * EXPERTS_PER_GROUP + top_e
    A = N * TOP_K
    e_flat = expert.reshape(A)
    tok = jnp.repeat(jnp.arange(N), TOP_K)
    w_flat = gate.reshape(A)
    counts = jnp.zeros((N_EXPERTS,), jnp.int32).at[e_flat].add(1)
    padded = (counts + MOE_BLOCK - 1) // MOE_BLOCK * MOE_BLOCK
    pad_end = jnp.cumsum(padded)
    pad_start = pad_end - padded
    start = jnp.cumsum(counts) - counts
    order = jnp.argsort(e_flat)
    e_sorted = e_flat[order]
    tok_sorted = tok[order]
    dest = pad_start[e_sorted] + (jnp.arange(A) - start[e_sorted])
    n_blocks = -(-A // MOE_BLOCK) + N_EXPERTS
    rows = n_blocks * MOE_BLOCK
    buf = jnp.zeros((rows, D), h.dtype).at[dest].set(xt[tok_sorted])
    blk_start = jnp.arange(n_blocks) * MOE_BLOCK
    blk_expert = jnp.minimum(jnp.sum(blk_start[:, None] >= pad_end[None, :], axis=1), N_EXPERTS - 1)

    def expert_block(args):
        xblk, e = args
        hid = jax.nn.silu(xblk @ w_gate[e]) * (xblk @ w_up[e])
        return hid @ w_down[e]

    yb = lax.map(expert_block, (buf.reshape(n_blocks, MOE_BLOCK, D), blk_expert)).reshape(rows, D)
    y = yb[dest] * w_flat[order][:, None].astype(h.dtype)
    out = jax.ops.segment_sum(y, tok_sorted, num_segments=N)
    return out.reshape(B, S, D)


def setup_inputs(seed: int = 0) -> dict:
    key = jax.random.key(seed)
    ks = jax.random.split(key, 28)
    L, D = DEPTH, D_MODEL
    beta = (8.0 * DEPTH) ** -0.25

    def nrm(k, shape, scale):
        return jax.random.normal(k, shape, jnp.float32) * scale

    return {
        "x": nrm(ks[0], (BATCH, SEQ, D), 1.0),
        "c": nrm(ks[1], (BATCH, D), 1.0),
        "w_ada": nrm(ks[2], (L, D, 6 * D), 0.1 * D ** -0.5),
        "b_ada": nrm(ks[3], (L, 6 * D), 0.02),
        "w_in": nrm(ks[4], (L, D, D_IN), D ** -0.5),
        "sinks": nrm(ks[5], (L, HA), 0.5),
        "cmp_pe_k": nrm(ks[6], (L, CMP_BLOCK, HEAD_DIM), 0.02),
        "cmp_w1_k": nrm(ks[7], (L, CMP_BLOCK * HEAD_DIM, CMP_HIDDEN), (CMP_BLOCK * HEAD_DIM) ** -0.5),
        "cmp_w2_k": nrm(ks[8], (L, CMP_HIDDEN, HEAD_DIM), CMP_HIDDEN ** -0.5),
        "cmp_pe_v": nrm(ks[9], (L, CMP_BLOCK, HEAD_DIM), 0.02),
        "cmp_w1_v": nrm(ks[10], (L, CMP_BLOCK * HEAD_DIM, CMP_HIDDEN), (CMP_BLOCK * HEAD_DIM) ** -0.5),
        "cmp_w2_v": nrm(ks[11], (L, CMP_HIDDEN, HEAD_DIM), CMP_HIDDEN ** -0.5),
        "mix_gain": 1.0 + nrm(ks[12], (L, D), 0.02),
        "w_out": nrm(ks[13], (L, D, D), beta * D ** -0.5),
        "ln1_g": 1.0 + nrm(ks[14], (L, D), 0.02),
        "ln1_b": nrm(ks[15], (L, D), 0.02),
        "w_rg": nrm(ks[16], (L, D, N_GROUPS), D ** -0.5),
        "b_rg": nrm(ks[17], (L, N_GROUPS), 0.01),
        "w_re": nrm(ks[18], (L, N_GROUPS, D, EXPERTS_PER_GROUP), D ** -0.5),
        "b_re": nrm(ks[19], (L, N_GROUPS, EXPERTS_PER_GROUP), 0.01),
        "w_gate": nrm(ks[20], (L, N_EXPERTS, D, D_EXPERT), D ** -0.5),
        "w_up": nrm(ks[21], (L, N_EXPERTS, D, D_EXPERT), D ** -0.5),
        "w_down": nrm(ks[22], (L, N_EXPERTS, D_EXPERT, D), beta * D_EXPERT ** -0.5),
        "ln2_g": 1.0 + nrm(ks[23], (L, D), 0.02),
        "ln2_b": nrm(ks[24], (L, D), 0.02),
    }


def reference(x, c, w_ada, b_ada, w_in, sinks, cmp_pe_k, cmp_w1_k, cmp_w2_k, cmp_pe_v, cmp_w1_v, cmp_w2_v,
              mix_gain, w_out, ln1_g, ln1_b, w_rg, b_rg, w_re, b_re, w_gate, w_up, w_down, ln2_g, ln2_b):
    alpha = (2.0 * DEPTH) ** 0.25
    cs = jax.nn.silu(c)
    for l in range(DEPTH):
        mod = (cs @ w_ada[l] + b_ada[l])[:, None, :]
        sh1, sc1, g1, sh2, sc2, g2 = jnp.split(mod, 6, axis=-1)
        h = layer_norm(x) * (1.0 + sc1) + sh1
        y = hybrid_mixer(h, w_in[l], sinks[l], cmp_pe_k[l], cmp_w1_k[l], cmp_w2_k[l],
                         cmp_pe_v[l], cmp_w1_v[l], cmp_w2_v[l], mix_gain[l], w_out[l])
        x = layer_norm(alpha * x + (1.0 + g1) * y, ln1_g[l], ln1_b[l])
        h = layer_norm(x) * (1.0 + sc2) + sh2
        y = hierarchical_moe(h, w_rg[l], b_rg[l], w_re[l], b_re[l], w_gate[l], w_up[l], w_down[l])
        x = layer_norm(alpha * x + (1.0 + g2) * y, ln2_g[l], ln2_b[l])
    return x
```

```python
import functools
import math

import numpy as np
import jax
import jax.numpy as jnp
from jax import lax
from jax.experimental import pallas as pl
from jax.experimental.pallas import tpu as pltpu

F32 = jnp.float32
BF16 = jnp.bfloat16
HIGHEST = lax.Precision.HIGHEST

D_MODEL = 2048
DEPTH = 2
HEAD_DIM = 64
HA = 8
HB = 8
HC = 16
KV_GROUPS = 2
REP = 4
SCALE = HEAD_DIM ** -0.5
WINDOW_A = 128
WINDOW_B = 512
CMP_BLOCK = 32
CMP_STRIDE = 16
CMP_HIDDEN = 128
SEL_BLOCK = 64
N_SELECT = 16
FORCE_SCORE = 1e4
N_GROUPS = 4
EXPERTS_PER_GROUP = 8
N_EXPERTS = 32
D_EXPERT = 512
ALPHA = (2.0 * DEPTH) ** 0.25
LANES = 128
VMEM_LIMIT = 48 * 1024 * 1024

SEG_SIZES = dict(qa=512, qb=512, ka=128, va=128, kbc=128, vbc=128, kbs=128, vbs=128,
                 kbw=128, vbw=128, gb=128, qc=1024, kc=1024, vc=1024)
SEG_ORDER = ("qa", "qb", "ka", "va", "kbc", "vbc", "kbs", "vbs", "kbw", "vbw", "gb", "qc", "kc", "vc")
SEG_OFF = {}
_o = 0
for _n in SEG_ORDER:
    SEG_OFF[_n] = _o
    _o += SEG_SIZES[_n]
PROJ_TN = 768
D_PROJ = -(-_o // PROJ_TN) * PROJ_TN
REF_SEGS = (("qa", 512), ("ka", 128), ("va", 128), ("qb", 512), ("kbc", 128), ("vbc", 128), ("kbs", 128),
            ("vbs", 128), ("kbw", 128), ("vbw", 128), ("gb", 24), ("qc", 1024), ("kc", 1024), ("vc", 1024))

SLOPES = [2.0 ** (-8.0 * (i + 1) / (HA + HB)) for i in range(HA + HB)]
STICK_EXIT = 105.0

ATT_TQ = 128
STICK_T = 256


def _cparams(sem):
    return pltpu.CompilerParams(dimension_semantics=sem, vmem_limit_bytes=VMEM_LIMIT)


def _dot_nt(a, b):
    return lax.dot_general(a, b, (((1,), (1,)), ((), ())), preferred_element_type=F32)


def _layer_norm(x, eps=1e-5):
    mu = jnp.mean(x, axis=-1, keepdims=True)
    xc = x - mu
    var = jnp.mean(xc * xc, axis=-1, keepdims=True)
    return xc * lax.rsqrt(var + eps)


def _ada_kernel(c_ref, w_ref, b_ref, o_ref):
    c = c_ref[...]
    cs = c * jax.nn.sigmoid(c)
    o_ref[0] = jnp.dot(cs, w_ref[0], preferred_element_type=F32, precision=HIGHEST) + b_ref[0]


def _ada_mod(c, w_ada, b_ada):
    B = c.shape[0]
    L, D, N6 = w_ada.shape
    cp = jnp.zeros((8, D), F32).at[:B].set(c)
    tn = 1024
    out = pl.pallas_call(
        _ada_kernel,
        grid=(L, N6 // tn),
        in_specs=[pl.BlockSpec((8, D), lambda l, j: (0, 0)),
                  pl.BlockSpec((1, D, tn), lambda l, j: (l, 0, j)),
                  pl.BlockSpec((1, 1, tn), lambda l, j: (l, 0, j))],
        out_specs=pl.BlockSpec((1, 8, tn), lambda l, j: (l, 0, j)),
        out_shape=jax.ShapeDtypeStruct((L, 8, N6), F32),
        compiler_params=_cparams(("arbitrary", "arbitrary")),
        name="ada_mod",
    )(cp, w_ada, b_ada.reshape(L, 1, N6))
    return out[:, :B]


def _inproj_kernel(x_ref, sc_ref, sh_ref, w_ref, o_ref, h_ref):
    @pl.when(pl.program_id(1) == 0)
    def _():
        h = _layer_norm(x_ref[...]) * (1.0 + sc_ref[0]) + sh_ref[0]
        h_ref[...] = h.astype(BF16)

    o_ref[...] = jnp.dot(h_ref[...], w_ref[...], preferred_element_type=F32).astype(BF16)


def _in_projection(x2d, sc, sh, w_packed, S):
    N, D = x2d.shape
    tm = 512 if S % 512 == 0 else S
    return pl.pallas_call(
        _inproj_kernel,
        grid=(N // tm, D_PROJ // PROJ_TN),
        in_specs=[pl.BlockSpec((tm, D), lambda i, j: (i, 0)),
                  pl.BlockSpec((1, 1, D), lambda i, j: (i * tm // S, 0, 0)),
                  pl.BlockSpec((1, 1, D), lambda i, j: (i * tm // S, 0, 0)),
                  pl.BlockSpec((D, PROJ_TN), lambda i, j: (0, j))],
        out_specs=pl.BlockSpec((tm, PROJ_TN), lambda i, j: (i, j)),
        out_shape=jax.ShapeDtypeStruct((N, D_PROJ), BF16),
        scratch_shapes=[pltpu.VMEM((tm, D), BF16)],
        compiler_params=_cparams(("arbitrary", "arbitrary")),
        name="ln_in_proj",
    )(x2d, sc, sh, w_packed)


def _pack_w_in(w_in_l):
    cols = {}
    off = 0
    for name, width in REF_SEGS:
        cols[name] = w_in_l[:, off:off + width]
        off += width
    parts = []
    for name in SEG_ORDER:
        wseg = cols[name]
        pad = SEG_SIZES[name] - wseg.shape[1]
        if pad:
            wseg = jnp.pad(wseg, ((0, 0), (0, pad)))
        parts.append(wseg)
    total = sum(SEG_SIZES.values())
    parts.append(jnp.zeros((w_in_l.shape[0], D_PROJ - total), w_in_l.dtype))
    return jnp.concatenate(parts, axis=1).astype(BF16)


def _compress_kernel(xa_ref, xb_ref, pe_ref, w1_ref, w2_ref, o_ref):
    half = CMP_STRIDE * HEAD_DIM
    for kv in range(2):
        xa = (xa_ref[kv, 0, 0].astype(F32) + pe_ref[kv, :, :half]).astype(BF16)
        xb = (xb_ref[kv, 0, 0].astype(F32) + pe_ref[kv, :, half:]).astype(BF16)
        hid = (jnp.dot(xa, w1_ref[kv, :half, :], preferred_element_type=F32)
               + jnp.dot(xb, w1_ref[kv, half:, :], preferred_element_type=F32))
        act = jax.nn.gelu(hid)
        o_ref[kv, 0, 0] = jnp.dot(act.astype(BF16), w2_ref[kv], preferred_element_type=F32).astype(BF16)


def _compress(proj, pe_kv, w1_kv, w2_kv, B, S):
    nchunk = S // CMP_STRIDE
    ncp = nchunk

    def chunks(name):
        t = proj[:, :, SEG_OFF[name]:SEG_OFF[name] + 128].reshape(B, nchunk, CMP_STRIDE, KV_GROUPS, HEAD_DIM)
        return t.transpose(0, 3, 1, 2, 4).reshape(B, KV_GROUPS, nchunk, CMP_STRIDE * HEAD_DIM)

    x16 = jnp.stack([chunks("kbc"), chunks("vbc")])
    xa = x16
    xb = jnp.concatenate([x16[:, :, :, 1:], jnp.zeros_like(x16[:, :, :, :1])], axis=3)
    pe = pe_kv.reshape(2, 1, CMP_BLOCK * HEAD_DIM)
    blk = (2, 1, 1, ncp, CMP_STRIDE * HEAD_DIM)
    return pl.pallas_call(
        _compress_kernel,
        grid=(B, KV_GROUPS),
        in_specs=[pl.BlockSpec(blk, lambda b, g: (0, b, g, 0, 0)),
                  pl.BlockSpec(blk, lambda b, g: (0, b, g, 0, 0)),
                  pl.BlockSpec((2, 1, CMP_BLOCK * HEAD_DIM), lambda b, g: (0, 0, 0)),
                  pl.BlockSpec((2, CMP_BLOCK * HEAD_DIM, CMP_HIDDEN), lambda b, g: (0, 0, 0)),
                  pl.BlockSpec((2, CMP_HIDDEN, HEAD_DIM), lambda b, g: (0, 0, 0))],
        out_specs=pl.BlockSpec((2, 1, 1, ncp, HEAD_DIM), lambda b, g: (0, b, g, 0, 0)),
        out_shape=jax.ShapeDtypeStruct((2, B, KV_GROUPS, ncp, HEAD_DIM), BF16),
        compiler_params=_cparams(("arbitrary", "arbitrary")),
        name="nsa_compress",
    )(xa, xb, pe, w1_kv.astype(BF16), w2_kv.astype(BF16))


def _pad_q(q_h, g):
    z = jnp.zeros_like(q_h)
    return jnp.concatenate([q_h, z], axis=1) if g == 0 else jnp.concatenate([z, q_h], axis=1)


def _online_update(s, m, l, acc, v_t):
    m_new = jnp.maximum(m, jnp.max(s, axis=-1, keepdims=True))
    a = jnp.exp(m - m_new)
    p = jnp.exp(s - m_new)
    l_new = a * l + jnp.sum(p, axis=-1, keepdims=True)
    acc_new = a * acc + jnp.dot(p.astype(BF16), v_t, preferred_element_type=F32)
    return m_new, l_new, acc_new


def _band_kernel(*refs, window, slopes, use_sink, gate_col, tq):
    if use_sink:
        sink_ref, q_ref, k_ref, v_ref, o_ref = refs
        g_ref = None
    else:
        q_ref, k_ref, v_ref, g_ref, o_ref = refs
    i = pl.program_id(1)
    n_back = window // tq
    row = lax.broadcasted_iota(jnp.int32, (tq, tq), 0)
    col = lax.broadcasted_iota(jnp.int32, (tq, tq), 1)
    rel = row - col
    q = q_ref[0]
    if g_ref is not None:
        gates = jax.nn.sigmoid(g_ref[0].astype(F32))
    for g in range(KV_GROUPS):
        heads = [g * REP + r for r in range(REP)]
        q_pads = [_pad_q(q[:, h * HEAD_DIM:(h + 1) * HEAD_DIM], g) for h in heads]
        if use_sink:
            state = [(jnp.full((tq, 1), sink_ref[h], F32), jnp.ones((tq, 1), F32), jnp.zeros((tq, LANES), F32))
                     for h in heads]
        else:
            state = [(jnp.full((tq, 1), -jnp.inf, F32), jnp.zeros((tq, 1), F32), jnp.zeros((tq, LANES), F32))
                     for h in heads]
        for j in range(n_back + 1):
            start = pl.multiple_of(jnp.maximum(i - j, 0) * tq, tq)
            k_t = k_ref[0, pl.ds(start, tq), :]
            v_t = v_ref[0, pl.ds(start, tq), :]
            dist = rel + j * tq
            distf = dist.astype(F32)
            keep = None
            if j == 0:
                keep = dist >= 0
            if j == n_back:
                far = dist < window
                keep = far if keep is None else keep & far
            for r, h in enumerate(heads):
                s = _dot_nt(q_pads[r], k_t) * SCALE - slopes[h] * distf
                if keep is not None:
                    s = jnp.where(keep, s, -jnp.inf)
                if j > 0:
                    s = jnp.where(i >= j, s, -jnp.inf)
                state[r] = _online_update(s, *state[r], v_t)
        for r, h in enumerate(heads):
            m, l, acc = state[r]
            out = acc[:, g * HEAD_DIM:(g + 1) * HEAD_DIM] / l
            if g_ref is not None:
                c = h * 3 + gate_col
                out = out * gates[:, c:c + 1]
            o_ref[0, :, h * HEAD_DIM:(h + 1) * HEAD_DIM] = out


def _band_attention(proj, qname, kname, vname, window, slopes, sinks, gate_col, B, S):
    tq = ATT_TQ
    kern = functools.partial(_band_kernel, window=window, slopes=slopes, use_sink=sinks is not None,
                             gate_col=gate_col, tq=tq)
    qb, kb, vb, gbk = SEG_OFF[qname] // 512, SEG_OFF[kname] // 128, SEG_OFF[vname] // 128, SEG_OFF["gb"] // 128
    specs = [pl.BlockSpec((1, tq, 512), lambda b, i: (b, i, qb)),
             pl.BlockSpec((1, S, 128), lambda b, i: (b, 0, kb)),
             pl.BlockSpec((1, S, 128), lambda b, i: (b, 0, vb))]
    args = [proj, proj, proj]
    if sinks is not None:
        specs = [pl.BlockSpec(memory_space=pltpu.SMEM)] + specs
        args = [sinks.astype(F32)] + args
    else:
        specs = specs + [pl.BlockSpec((1, tq, 128), lambda b, i: (b, i, gbk))]
        args = args + [proj]
    return pl.pallas_call(
        kern,
        grid=(B, S // tq),
        in_specs=specs,
        out_specs=pl.BlockSpec((1, tq, 512), lambda b, i: (b, i, 0)),
        out_shape=jax.ShapeDtypeStruct((B, S, 512), F32),
        compiler_params=_cparams(("arbitrary", "arbitrary")),
        name="band_attn_w%d" % window,
    )(*args)


def _cmp_select_kernel(q_ref, kc_ref, vc_ref, g_ref, ov_ref, o_ref, sel_ref, cnt_ref, *, tq, n_pick):
    i = pl.program_id(1)
    ncp = kc_ref.shape[3]
    q = q_ref[0]
    gates = jax.nn.sigmoid(g_ref[0].astype(F32))
    t_col = i * tq + lax.broadcasted_iota(jnp.int32, (tq, 1), 0)
    n_idx = lax.broadcasted_iota(jnp.int32, (tq, ncp), 1)
    dist_c = t_col - (n_idx * CMP_STRIDE + (CMP_BLOCK - 1))
    distf = dist_c.astype(F32)
    mask_c = dist_c >= 0
    j_idx = lax.broadcasted_iota(jnp.int32, (tq, LANES), 1)
    cur = t_col // SEL_BLOCK
    forced = (j_idx == 0) | (j_idx == cur) | (j_idx == cur - 1)
    valid = j_idx * SEL_BLOCK <= t_col
    for g in range(KV_GROUPS):
        kc = kc_ref[0, 0, g]
        vc = vc_ref[0, 0, g]
        psum = jnp.zeros((tq, ncp), F32)
        for r in range(REP):
            h = g * REP + r
            q_h = q[:, h * HEAD_DIM:(h + 1) * HEAD_DIM]
            s = _dot_nt(q_h, kc) * SCALE - SLOPES[HA + h] * distf
            s = jnp.where(mask_c, s, -jnp.inf)
            m = jnp.max(s, axis=-1, keepdims=True)
            m = jnp.where(m == -jnp.inf, 0.0, m)
            e = jnp.exp(s - m)
            den = jnp.sum(e, axis=-1, keepdims=True)
            p = e / jnp.maximum(den, 1e-30)
            psum = psum + p
            out = jnp.dot(p.astype(BF16), vc, preferred_element_type=F32)
            c = h * 3
            o_ref[0, :, h * HEAD_DIM:(h + 1) * HEAD_DIM] = out * gates[:, c:c + 1]
        p_hi = psum.astype(BF16)
        p_lo = (psum - p_hi.astype(F32)).astype(BF16)
        imp = (jnp.dot(p_hi, ov_ref[...], preferred_element_type=F32)
               + jnp.dot(p_lo, ov_ref[...], preferred_element_type=F32))
        imp = jnp.where(valid, jnp.where(forced, FORCE_SCORE, imp), -1.0)

        def pick(_, carry):
            imp, sel = carry
            mx = jnp.max(imp, axis=-1, keepdims=True)
            first = jnp.min(jnp.where(imp == mx, j_idx, LANES), axis=-1, keepdims=True)
            hit = j_idx == first
            return jnp.where(hit, -2.0, imp), jnp.where(hit, 1.0, sel)

        _, sel = lax.fori_loop(0, n_pick, pick, (imp, jnp.zeros((tq, LANES), F32)))
        sel_ref[0, g] = sel.astype(BF16)
        cnt_ref[0, 0, g:g + 1, :] = jnp.sum(sel, axis=0, keepdims=True)


def _overlap_matrix(ncp):
    n = np.arange(ncp)[:, None]
    j = np.arange(LANES)[None, :]
    start = n * CMP_STRIDE
    end = start + CMP_BLOCK - 1
    return ((end >= j * SEL_BLOCK) & (start < j * SEL_BLOCK + SEL_BLOCK)).astype(np.float32)


def _cmp_select(proj, kvc, B, S):
    tq = ATT_TQ
    nq = S // tq
    ncp = kvc.shape[3]
    n_pick = min(N_SELECT, S // SEL_BLOCK)
    qb, gbk = SEG_OFF["qb"] // 512, SEG_OFF["gb"] // 128
    ov = jnp.asarray(_overlap_matrix(ncp), BF16)
    kern = functools.partial(_cmp_select_kernel, tq=tq, n_pick=n_pick)
    kv_blk = (1, 1, KV_GROUPS, ncp, HEAD_DIM)
    return pl.pallas_call(
        kern,
        grid=(B, nq),
        in_specs=[pl.BlockSpec((1, tq, 512), lambda b, i: (b, i, qb)),
                  pl.BlockSpec(kv_blk, lambda b, i: (0, b, 0, 0, 0)),
                  pl.BlockSpec(kv_blk, lambda b, i: (1, b, 0, 0, 0)),
                  pl.BlockSpec((1, tq, 128), lambda b, i: (b, i, gbk)),
                  pl.BlockSpec((ncp, LANES), lambda b, i: (0, 0))],
        out_specs=[pl.BlockSpec((1, tq, 512), lambda b, i: (b, i, 0)),
                   pl.BlockSpec((1, KV_GROUPS, tq, LANES), lambda b, i: (b, 0, i, 0)),
                   pl.BlockSpec((1, 1, KV_GROUPS, LANES), lambda b, i: (b, i, 0, 0))],
        out_shape=[jax.ShapeDtypeStruct((B, S, 512), F32),
                   jax.ShapeDtypeStruct((B, KV_GROUPS, S, LANES), BF16),
                   jax.ShapeDtypeStruct((B, nq, KV_GROUPS, LANES), F32)],
        compiler_params=_cparams(("arbitrary", "arbitrary")),
        name="nsa_cmp_select",
    )(proj, kvc, kvc, proj, ov)


def _sel_kernel(flag_ref, q_ref, k_ref, v_ref, sel_ref, g_ref, o_ref, m_s, l_s, acc_s, *, tq, nq):
    b = pl.program_id(0)
    i = pl.program_id(1)
    q = q_ref[0]
    gates = jax.nn.sigmoid(g_ref[0].astype(F32))
    row = lax.broadcasted_iota(jnp.int32, (tq, tq), 0)
    col = lax.broadcasted_iota(jnp.int32, (tq, tq), 1)
    rel = row - col
    blk_row = lax.broadcasted_iota(jnp.int32, (LANES, tq), 0)
    blk_col = lax.broadcasted_iota(jnp.int32, (LANES, tq), 1) // SEL_BLOCK
    per_tile = tq // SEL_BLOCK
    for g in range(KV_GROUPS):
        sel_g = sel_ref[0, g]
        q_pads = [_pad_q(q[:, (g * REP + r) * HEAD_DIM:(g * REP + r + 1) * HEAD_DIM], g) for r in range(REP)]
        m_s[...] = jnp.full(m_s.shape, -jnp.inf, F32)
        l_s[...] = jnp.zeros(l_s.shape, F32)
        acc_s[...] = jnp.zeros(acc_s.shape, F32)

        def tile(kt, q_pads=q_pads, sel_g=sel_g, g=g):
            start = pl.multiple_of(kt * tq, tq)
            k_t = k_ref[0, pl.ds(start, tq), :]
            v_t = v_ref[0, pl.ds(start, tq), :]
            expand = jnp.where(blk_row == kt * per_tile + blk_col, 1.0, 0.0).astype(BF16)
            picked = jnp.dot(sel_g, expand, preferred_element_type=F32)
            dist = rel + (i - kt) * tq
            keep = (dist >= 0) & (picked > 0.5)
            distf = dist.astype(F32)
            for r in range(REP):
                s = _dot_nt(q_pads[r], k_t) * SCALE - SLOPES[HA + g * REP + r] * distf
                s = jnp.where(keep, s, -jnp.inf)
                m, l, acc = _online_update(s, m_s[r], l_s[r], acc_s[r], v_t)
                m_s[r] = m
                l_s[r] = l
                acc_s[r] = acc

        tile(i)

        def body(j, _, tile=tile, g=g):
            kt = i - 1 - j

            @pl.when(flag_ref[((b * KV_GROUPS + g) * nq + i) * nq + kt] > 0)
            def _():
                tile(kt)

            return 0

        lax.fori_loop(0, i, body, 0)
        for r in range(REP):
            h = g * REP + r
            out = acc_s[r][:, g * HEAD_DIM:(g + 1) * HEAD_DIM] / l_s[r]
            c = h * 3 + 1
            o_ref[0, :, h * HEAD_DIM:(h + 1) * HEAD_DIM] = out * gates[:, c:c + 1]


def _selected_attention(proj, sel, flags, B, S):
    tq = ATT_TQ
    nq = S // tq
    qb, kb, vb, gbk = SEG_OFF["qb"] // 512, SEG_OFF["kbs"] // 128, SEG_OFF["vbs"] // 128, SEG_OFF["gb"] // 128
    kern = functools.partial(_sel_kernel, tq=tq, nq=nq)
    grid_spec = pltpu.PrefetchScalarGridSpec(
        num_scalar_prefetch=1,
        grid=(B, nq),
        in_specs=[pl.BlockSpec((1, tq, 512), lambda b, i, f: (b, i, qb)),
                  pl.BlockSpec((1, S, 128), lambda b, i, f: (b, 0, kb)),
                  pl.BlockSpec((1, S, 128), lambda b, i, f: (b, 0, vb)),
                  pl.BlockSpec((1, KV_GROUPS, tq, LANES), lambda b, i, f: (b, 0, i, 0)),
                  pl.BlockSpec((1, tq, 128), lambda b, i, f: (b, i, gbk))],
        out_specs=pl.BlockSpec((1, tq, 512), lambda b, i, f: (b, i, 0)),
        scratch_shapes=[pltpu.VMEM((REP, tq, 1), F32), pltpu.VMEM((REP, tq, 1), F32),
                        pltpu.VMEM((REP, tq, LANES), F32)],
    )
    return pl.pallas_call(
        kern,
        grid_spec=grid_spec,
        out_shape=jax.ShapeDtypeStruct((B, S, 512), F32),
        compiler_params=_cparams(("arbitrary", "arbitrary")),
        name="nsa_selected",
    )(flags, proj, proj, proj, sel, proj)


def _stick_kernel(q_ref, k_ref, v_ref, o_ref, *, t):
    i = pl.program_id(2)
    lane = lax.broadcasted_iota(jnp.int32, (t, LANES), 1)
    row = lax.broadcasted_iota(jnp.int32, (t, t), 0)
    col = lax.broadcasted_iota(jnp.int32, (t, t), 1)
    causal = col < row
    suffix = jnp.where(row >= col, 1.0, 0.0).astype(BF16)
    q = q_ref[0]
    outs = []
    for hh in range(2):
        in_head = (lane >= hh * HEAD_DIM) & (lane < (hh + 1) * HEAD_DIM)
        q_h = jnp.where(in_head, q, jnp.zeros_like(q))

        def tile(kt, carry, acc, masked, q_h=q_h):
            start = pl.multiple_of(kt * t, t)
            k_t = k_ref[0, pl.ds(start, t), :]
            v_t = v_ref[0, pl.ds(start, t), :]
            z = _dot_nt(q_h, k_t) * SCALE
            log_1mb = -(jnp.maximum(z, 0.0) + jnp.log(1.0 + jnp.exp(-jnp.abs(z))))
            if masked:
                log_1mb = jnp.where(causal, log_1mb, 0.0)
            hi = log_1mb.astype(BF16)
            lo = (log_1mb - hi.astype(F32)).astype(BF16)
            inc = (jnp.dot(hi, suffix, preferred_element_type=F32)
                   + jnp.dot(lo, suffix, preferred_element_type=F32))
            a = jnp.exp(z + inc + carry)
            if masked:
                a = jnp.where(causal, a, 0.0)
            acc = acc + jnp.dot(a.astype(BF16), v_t, preferred_element_type=F32)
            return carry + inc[:, 0:1], acc

        carry, acc = tile(i, jnp.zeros((t, 1), F32), jnp.zeros((t, LANES), F32), True)

        def cond(state):
            kt, go, _, _ = state
            return (kt >= 0) & (go > 0)

        def body(state, tile=tile):
            kt, _, carry, acc = state
            carry, acc = tile(kt, carry, acc, False)
            go = (jnp.max(carry) > -STICK_EXIT).astype(jnp.int32)
            return kt - 1, go, carry, acc

        go0 = (jnp.max(carry) > -STICK_EXIT).astype(jnp.int32)
        _, _, _, acc = lax.while_loop(cond, body, (i - 1, go0, carry, acc))
        outs.append(acc)
    o_ref[0] = jnp.where(lane < HEAD_DIM, outs[0], outs[1])


def _stick_attention(proj, B, S):
    t = min(STICK_T, S)
    qb, kb, vb = SEG_OFF["qc"] // 128, SEG_OFF["kc"] // 128, SEG_OFF["vc"] // 128
    npair = HC // 2
    return pl.pallas_call(
        functools.partial(_stick_kernel, t=t),
        grid=(B, npair, S // t),
        in_specs=[pl.BlockSpec((1, t, 128), lambda b, p, i: (b, i, qb + p)),
                  pl.BlockSpec((1, S, 128), lambda b, p, i: (b, 0, kb + p)),
                  pl.BlockSpec((1, S, 128), lambda b, p, i: (b, 0, vb + p))],
        out_specs=pl.BlockSpec((1, t, 128), lambda b, p, i: (b, i, p)),
        out_shape=jax.ShapeDtypeStruct((B, S, HC * HEAD_DIM), F32),
        compiler_params=_cparams(("arbitrary", "arbitrary", "arbitrary")),
        name="stick_breaking",
    )(proj, proj, proj)


def _group_rms(o, eps=1e-6):
    return o * lax.rsqrt(jnp.mean(o * o, axis=-1, keepdims=True) + eps)


def _outproj_kernel(oa_ref, ob1_ref, ob2_ref, ob3_ref, oc_ref, x_ref, gain_ref, w_ref, g1_ref, lg_ref, lb_ref,
                    sc2_ref, sh2_ref, wr_ref, br_ref, x1_ref, h2_ref, route_ref):
    o_b = ob1_ref[0] + ob2_ref[0] + ob3_ref[0]
    merged = jnp.concatenate([_group_rms(oa_ref[0]), _group_rms(o_b), _group_rms(oc_ref[0])], axis=-1)
    merged = (merged * gain_ref[...]).astype(BF16)
    y = jnp.dot(merged, w_ref[...], preferred_element_type=F32)
    x1 = _layer_norm(ALPHA * x_ref[0] + (1.0 + g1_ref[0]) * y) * lg_ref[...] + lb_ref[...]
    x1_ref[0] = x1
    h2 = _layer_norm(x1) * (1.0 + sc2_ref[0]) + sh2_ref[0]
    h2_ref[0] = h2.astype(BF16)
    logits = jnp.dot(h2, wr_ref[...], preferred_element_type=F32, precision=HIGHEST) + br_ref[...]
    tm = logits.shape[0]
    lane = lax.broadcasted_iota(jnp.int32, (tm, LANES), 1)
    lg = jnp.where(lane < N_GROUPS, logits, -jnp.inf)
    mg = jnp.max(lg, axis=-1, keepdims=True)
    p_group = 1.0 / jnp.sum(jnp.exp(lg - mg), axis=-1, keepdims=True)
    g_sel = jnp.min(jnp.where(lg == mg, lane, LANES), axis=-1, keepdims=True)
    e_idx = lane - N_GROUPS
    in_grp = (e_idx >= g_sel * EXPERTS_PER_GROUP) & (e_idx < (g_sel + 1) * EXPERTS_PER_GROUP)
    le = jnp.where(in_grp, logits, -jnp.inf)
    m1 = jnp.max(le, axis=-1, keepdims=True)
    i1 = jnp.min(jnp.where(le == m1, e_idx, LANES), axis=-1, keepdims=True)
    le2 = jnp.where(e_idx == i1, -jnp.inf, le)
    m2 = jnp.max(le2, axis=-1, keepdims=True)
    i2 = jnp.min(jnp.where(le2 == m2, e_idx, LANES), axis=-1, keepdims=True)
    den = jnp.sum(jnp.exp(le - m1), axis=-1, keepdims=True)
    p1 = 1.0 / den
    p2 = jnp.exp(m2 - m1) / den
    gate1 = p_group * p1 / (p1 + p2)
    gate2 = p_group * p2 / (p1 + p2)
    route = jnp.where(lane == 0, i1.astype(F32),
                      jnp.where(lane == 1, i2.astype(F32),
                                jnp.where(lane == 2, gate1, jnp.where(lane == 3, gate2, 0.0))))
    route_ref[0] = route


def _out_projection(o_a, o_b1, o_b2, o_b3, o_c, x, gain, w_out, g1, ln_g, ln_b, sc2, sh2, w_r, b_r, B, S):
    D = D_MODEL
    tm = 256
    row = lambda b, i: (b, i, 0)
    vec = lambda b, i: (0, 0)
    bvec = lambda b, i: (b, 0, 0)
    return pl.pallas_call(
        _outproj_kernel,
        grid=(B, S // tm),
        in_specs=[pl.BlockSpec((1, tm, 512), row), pl.BlockSpec((1, tm, 512), row),
                  pl.BlockSpec((1, tm, 512), row), pl.BlockSpec((1, tm, 512), row),
                  pl.BlockSpec((1, tm, 1024), row), pl.BlockSpec((1, tm, D), row),
                  pl.BlockSpec((1, D), vec), pl.BlockSpec((D, D), vec),
                  pl.BlockSpec((1, 1, D), bvec), pl.BlockSpec((1, D), vec), pl.BlockSpec((1, D), vec),
                  pl.BlockSpec((1, 1, D), bvec), pl.BlockSpec((1, 1, D), bvec),
                  pl.BlockSpec((D, LANES), vec), pl.BlockSpec((1, LANES), vec)],
        out_specs=[pl.BlockSpec((1, tm, D), row), pl.BlockSpec((1, tm, D), row),
                   pl.BlockSpec((1, tm, LANES), row)],
        out_shape=[jax.ShapeDtypeStruct((B, S, D), F32), jax.ShapeDtypeStruct((B, S, D), BF16),
                   jax.ShapeDtypeStruct((B, S, LANES), F32)],
        compiler_params=_cparams(("arbitrary", "arbitrary")),
        name="out_proj_norm_route",
    )(o_a, o_b1, o_b2, o_b3, o_c, x, gain, w_out, g1, ln_g, ln_b, sc2, sh2, w_r, b_r)


def _expert_kernel(be_ref, nb_ref, x_ref, gw_ref, wg_ref, wu_ref, wd_ref, y_ref):
    @pl.when(pl.program_id(0) < nb_ref[0])
    def _():
        x = x_ref[...]
        hg = jnp.dot(x, wg_ref[0], preferred_element_type=F32)
        hu = jnp.dot(x, wu_ref[0], preferred_element_type=F32)
        hid = (hg * jax.nn.sigmoid(hg)) * hu
        y = jnp.dot(hid.astype(BF16), wd_ref[0], preferred_element_type=F32)
        y_ref[...] = y * gw_ref[...]

    @pl.when(pl.program_id(0) >= nb_ref[0])
    def _():
        y_ref[...] = jnp.zeros(y_ref.shape, F32)


def _expert_mlp(blk_expert, n_used, xs, row_gate, w_gate, w_up, w_down, bm):
    rows, D = xs.shape
    nblk = rows // bm
    grid_spec = pltpu.PrefetchScalarGridSpec(
        num_scalar_prefetch=2,
        grid=(nblk,),
        in_specs=[pl.BlockSpec((bm, D), lambda i, be, nb: (i, 0)),
                  pl.BlockSpec((bm, 1), lambda i, be, nb: (i, 0)),
                  pl.BlockSpec((1, D, D_EXPERT), lambda i, be, nb: (be[i], 0, 0)),
                  pl.BlockSpec((1, D, D_EXPERT), lambda i, be, nb: (be[i], 0, 0)),
                  pl.BlockSpec((1, D_EXPERT, D), lambda i, be, nb: (be[i], 0, 0))],
        out_specs=pl.BlockSpec((bm, D), lambda i, be, nb: (i, 0)),
    )
    return pl.pallas_call(
        _expert_kernel,
        grid_spec=grid_spec,
        out_shape=jax.ShapeDtypeStruct((rows, D), F32),
        compiler_params=_cparams(("arbitrary",)),
        name="expert_mlp",
    )(blk_expert, n_used, xs, row_gate, w_gate, w_up, w_down)


def _postnorm_kernel(x_ref, y_ref, g2_ref, lg_ref, lb_ref, o_ref):
    o_ref[0] = _layer_norm(ALPHA * x_ref[0] + (1.0 + g2_ref[0]) * y_ref[0]) * lg_ref[...] + lb_ref[...]


def _post_norm(x1, y, g2, ln_g, ln_b, B, S):
    D = D_MODEL
    tm = 512 if S % 512 == 0 else S
    row = lambda b, i: (b, i, 0)
    return pl.pallas_call(
        _postnorm_kernel,
        grid=(B, S // tm),
        in_specs=[pl.BlockSpec((1, tm, D), row), pl.BlockSpec((1, tm, D), row),
                  pl.BlockSpec((1, 1, D), lambda b, i: (b, 0, 0)),
                  pl.BlockSpec((1, D), lambda b, i: (0, 0)), pl.BlockSpec((1, D), lambda b, i: (0, 0))],
        out_specs=pl.BlockSpec((1, tm, D), row),
        out_shape=jax.ShapeDtypeStruct((B, S, D), F32),
        compiler_params=_cparams(("arbitrary", "arbitrary")),
        name="moe_post_norm",
    )(x1, y, g2, ln_g, ln_b)


MOE_BM = 256


def _moe(h2, route, w_gate, w_up, w_down, B, S):
    D = D_MODEL
    N = B * S
    A = 2 * N
    bm = MOE_BM
    r = route.reshape(N, LANES)
    e_flat = r[:, 0:2].astype(jnp.int32).reshape(A)
    w_flat = r[:, 2:4].reshape(A)
    onehot = (e_flat[:, None] == jnp.arange(N_EXPERTS)[None, :]).astype(jnp.int32)
    csum = jnp.cumsum(onehot, axis=0)
    counts = csum[-1]
    rank = jnp.sum((csum - onehot) * onehot, axis=1)
    padded = (counts + bm - 1) // bm * bm
    pad_end = jnp.cumsum(padded)
    pad_start = pad_end - padded
    dest = pad_start[e_flat] + rank
    nblk = -(-A // bm) + N_EXPERTS
    rows = nblk * bm
    row_tok = jnp.zeros((rows,), jnp.int32).at[dest].set(jnp.arange(A, dtype=jnp.int32) // 2)
    row_gate = jnp.zeros((rows,), F32).at[dest].set(w_flat)
    blk_start = jnp.arange(nblk) * bm
    blk_expert = jnp.minimum(jnp.sum(blk_start[:, None] >= pad_end[None, :], axis=1), N_EXPERTS - 1).astype(jnp.int32)
    n_used = (pad_end[-1] // bm).astype(jnp.int32).reshape(1)
    xs = h2.reshape(N, D)[row_tok]
    yb = _expert_mlp(blk_expert, n_used, xs, row_gate.reshape(rows, 1), w_gate, w_up, w_down, bm)
    d2 = dest.reshape(N, 2)
    y = yb[d2[:, 0]] + yb[d2[:, 1]]
    return y.reshape(B, S, D)


def _selection_flags(cnt, B, S):
    nq = S // ATT_TQ
    per_tile = ATT_TQ // SEL_BLOCK
    c = cnt.transpose(0, 2, 1, 3)[..., :nq * per_tile]
    f = c.reshape(B, KV_GROUPS, nq, nq, per_tile).sum(-1) > 0
    return f.astype(jnp.int32).reshape(-1)


def kernel(x, c, w_ada, b_ada, w_in, sinks, cmp_pe_k, cmp_w1_k, cmp_w2_k, cmp_pe_v, cmp_w1_v, cmp_w2_v, mix_gain, w_out, ln1_g, ln1_b, w_rg, b_rg, w_re, b_re, w_gate, w_up, w_down, ln2_g, ln2_b):
    B, S, D = x.shape
    L = w_ada.shape[0]
    mod = _ada_mod(c, w_ada, b_ada)
    for l in range(L):
        sh1, sc1, g1, sh2, sc2, g2 = [mod[l, :, k * D:(k + 1) * D].reshape(B, 1, D) for k in range(6)]
        proj = _in_projection(x.reshape(B * S, D), sc1, sh1, _pack_w_in(w_in[l]), S).reshape(B, S, D_PROJ)
        o_a = _band_attention(proj, "qa", "ka", "va", WINDOW_A, SLOPES[:HA], sinks[l], 0, B, S)
        kvc = _compress(proj, jnp.stack([cmp_pe_k[l], cmp_pe_v[l]]), jnp.stack([cmp_w1_k[l], cmp_w1_v[l]]),
                        jnp.stack([cmp_w2_k[l], cmp_w2_v[l]]), B, S)
        o_cmp, sel, cnt = _cmp_select(proj, kvc, B, S)
        o_slc = _selected_attention(proj, sel, _selection_flags(cnt, B, S), B, S)
        o_win = _band_attention(proj, "qb", "kbw", "vbw", WINDOW_B, SLOPES[HA:], None, 2, B, S)
        o_c = _stick_attention(proj, B, S)
        w_r = jnp.concatenate([w_rg[l], w_re[l].transpose(1, 0, 2).reshape(D, N_EXPERTS),
                               jnp.zeros((D, LANES - N_GROUPS - N_EXPERTS), F32)], axis=1)
        b_r = jnp.concatenate([b_rg[l], b_re[l].reshape(N_EXPERTS),
                               jnp.zeros((LANES - N_GROUPS - N_EXPERTS,), F32)]).reshape(1, LANES)
        x1, h2, route = _out_projection(o_a, o_cmp, o_slc, o_win, o_c, x, mix_gain[l].reshape(1, D),
                                        w_out[l].astype(BF16), g1, ln1_g[l].reshape(1, D), ln1_b[l].reshape(1, D),
                                        sc2, sh2, w_r, b_r, B, S)
        y = _moe(h2, route, w_gate[l].astype(BF16), w_up[l].astype(BF16), w_down[l].astype(BF16), B, S)
        x = _post_norm(x1, y, g2, ln2_g[l].reshape(1, D), ln2_b[l].reshape(1, D), B, S)
    return x
```

```python
import functools

import numpy as np
import jax
import jax.numpy as jnp
from jax import lax
from jax.experimental import pallas as pl
from jax.experimental.pallas import tpu as pltpu

F32 = jnp.float32
BF16 = jnp.bfloat16
HIGHEST = lax.Precision.HIGHEST

D_MODEL = 2048
DEPTH = 2
HEAD_DIM = 64
HA = 8
HB = 8
HC = 16
KV_GROUPS = 2
REP = 4
N_STACK = KV_GROUPS * REP
SCALE = HEAD_DIM ** -0.5
WINDOW_A = 128
WINDOW_B = 512
CMP_BLOCK = 32
CMP_STRIDE = 16
CMP_HIDDEN = 128
SEL_BLOCK = 64
N_SELECT = 16
FORCE_SCORE = 1e4
N_GROUPS = 4
EXPERTS_PER_GROUP = 8
N_EXPERTS = 32
D_EXPERT = 512
ALPHA = (2.0 * DEPTH) ** 0.25
LANES = 128
VMEM_LIMIT = 48 * 1024 * 1024

SEG_SIZES = dict(qa=512, qb=512, ka=128, va=128, kbc=128, vbc=128, kbs=128, vbs=128,
                 kbw=128, vbw=128, gb=128, qc=1024, kc=1024, vc=1024)
SEG_ORDER = ("qa", "qb", "ka", "va", "kbc", "vbc", "kbs", "vbs", "kbw", "vbw", "gb", "qc", "kc", "vc")
SEG_OFF = {}
_o = 0
for _n in SEG_ORDER:
    SEG_OFF[_n] = _o
    _o += SEG_SIZES[_n]
PROJ_TN = 768
D_PROJ = -(-_o // PROJ_TN) * PROJ_TN
REF_SEGS = (("qa", 512), ("ka", 128), ("va", 128), ("qb", 512), ("kbc", 128), ("vbc", 128), ("kbs", 128),
            ("vbs", 128), ("kbw", 128), ("vbw", 128), ("gb", 24), ("qc", 1024), ("kc", 1024), ("vc", 1024))

SLOPES = [2.0 ** (-8.0 * (i + 1) / (HA + HB)) for i in range(HA + HB)]
STICK_EXIT = 105.0

ATT_TQ = 128
STICK_T = 256


def _cparams(sem):
    return pltpu.CompilerParams(dimension_semantics=sem, vmem_limit_bytes=VMEM_LIMIT)


def _dot_nt(a, b):
    return lax.dot_general(a, b, (((1,), (1,)), ((), ())), preferred_element_type=F32)


def _dot_tn(a, b):
    return lax.dot_general(a, b, (((0,), (0,)), ((), ())), preferred_element_type=F32)


def _layer_norm(x, eps=1e-5):
    mu = jnp.mean(x, axis=-1, keepdims=True)
    xc = x - mu
    var = jnp.mean(xc * xc, axis=-1, keepdims=True)
    return xc * lax.rsqrt(var + eps)


def _ada_kernel(c_ref, w_ref, b_ref, o_ref):
    c = c_ref[...]
    cs = c * jax.nn.sigmoid(c)
    o_ref[0] = jnp.dot(cs, w_ref[0], preferred_element_type=F32, precision=HIGHEST) + b_ref[0]


def _ada_mod(c, w_ada, b_ada):
    B = c.shape[0]
    L, D, N6 = w_ada.shape
    cp = jnp.zeros((8, D), F32).at[:B].set(c)
    tn = 1024
    out = pl.pallas_call(
        _ada_kernel,
        grid=(L, N6 // tn),
        in_specs=[pl.BlockSpec((8, D), lambda l, j: (0, 0)),
                  pl.BlockSpec((1, D, tn), lambda l, j: (l, 0, j)),
                  pl.BlockSpec((1, 1, tn), lambda l, j: (l, 0, j))],
        out_specs=pl.BlockSpec((1, 8, tn), lambda l, j: (l, 0, j)),
        out_shape=jax.ShapeDtypeStruct((L, 8, N6), F32),
        compiler_params=_cparams(("arbitrary", "arbitrary")),
        name="ada_mod",
    )(cp, w_ada, b_ada.reshape(L, 1, N6))
    return out[:, :B]


def _inproj_kernel(x_ref, sc_ref, sh_ref, w_ref, o_ref, h_ref):
    @pl.when(pl.program_id(1) == 0)
    def _():
        h = _layer_norm(x_ref[...]) * (1.0 + sc_ref[0]) + sh_ref[0]
        h_ref[...] = h.astype(BF16)

    o_ref[...] = jnp.dot(h_ref[...], w_ref[...], preferred_element_type=F32).astype(BF16)


def _in_projection(x2d, sc, sh, w_packed, S):
    N, D = x2d.shape
    tm = 512 if S % 512 == 0 else S
    return pl.pallas_call(
        _inproj_kernel,
        grid=(N // tm, D_PROJ // PROJ_TN),
        in_specs=[pl.BlockSpec((tm, D), lambda i, j: (i, 0)),
                  pl.BlockSpec((1, 1, D), lambda i, j: (i * tm // S, 0, 0)),
                  pl.BlockSpec((1, 1, D), lambda i, j: (i * tm // S, 0, 0)),
                  pl.BlockSpec((D, PROJ_TN), lambda i, j: (0, j))],
        out_specs=pl.BlockSpec((tm, PROJ_TN), lambda i, j: (i, j)),
        out_shape=jax.ShapeDtypeStruct((N, D_PROJ), BF16),
        scratch_shapes=[pltpu.VMEM((tm, D), BF16)],
        compiler_params=_cparams(("arbitrary", "arbitrary")),
        name="ln_in_proj",
    )(x2d, sc, sh, w_packed)


def _pack_w_in(w_in_l):
    cols = {}
    off = 0
    for name, width in REF_SEGS:
        cols[name] = w_in_l[:, off:off + width]
        off += width
    parts = []
    for name in SEG_ORDER:
        wseg = cols[name]
        pad = SEG_SIZES[name] - wseg.shape[1]
        if pad:
            wseg = jnp.pad(wseg, ((0, 0), (0, pad)))
        parts.append(wseg)
    total = sum(SEG_SIZES.values())
    parts.append(jnp.zeros((w_in_l.shape[0], D_PROJ - total), w_in_l.dtype))
    return jnp.concatenate(parts, axis=1).astype(BF16)


def _compress_kernel(xa_ref, xb_ref, pe_ref, w1_ref, w2_ref, o_ref):
    half = CMP_STRIDE * HEAD_DIM
    for kv in range(2):
        xa = (xa_ref[kv, 0, 0].astype(F32) + pe_ref[kv, :, :half]).astype(BF16)
        xb = (xb_ref[kv, 0, 0].astype(F32) + pe_ref[kv, :, half:]).astype(BF16)
        hid = (jnp.dot(xa, w1_ref[kv, :half, :], preferred_element_type=F32)
               + jnp.dot(xb, w1_ref[kv, half:, :], preferred_element_type=F32))
        act = jax.nn.gelu(hid)
        o_ref[kv, 0, 0] = jnp.dot(act.astype(BF16), w2_ref[kv], preferred_element_type=F32).astype(BF16)


def _compress(proj, pe_kv, w1_kv, w2_kv, B, S):
    nchunk = S // CMP_STRIDE
    ncp = nchunk

    def chunks(name):
        t = proj[:, :, SEG_OFF[name]:SEG_OFF[name] + 128].reshape(B, nchunk, CMP_STRIDE, KV_GROUPS, HEAD_DIM)
        return t.transpose(0, 3, 1, 2, 4).reshape(B, KV_GROUPS, nchunk, CMP_STRIDE * HEAD_DIM)

    x16 = jnp.stack([chunks("kbc"), chunks("vbc")])
    xa = x16
    xb = jnp.concatenate([x16[:, :, :, 1:], jnp.zeros_like(x16[:, :, :, :1])], axis=3)
    pe = pe_kv.reshape(2, 1, CMP_BLOCK * HEAD_DIM)
    blk = (2, 1, 1, ncp, CMP_STRIDE * HEAD_DIM)
    out = pl.pallas_call(
        _compress_kernel,
        grid=(B, KV_GROUPS),
        in_specs=[pl.BlockSpec(blk, lambda b, g: (0, b, g, 0, 0)),
                  pl.BlockSpec(blk, lambda b, g: (0, b, g, 0, 0)),
                  pl.BlockSpec((2, 1, CMP_BLOCK * HEAD_DIM), lambda b, g: (0, 0, 0)),
                  pl.BlockSpec((2, CMP_BLOCK * HEAD_DIM, CMP_HIDDEN), lambda b, g: (0, 0, 0)),
                  pl.BlockSpec((2, CMP_HIDDEN, HEAD_DIM), lambda b, g: (0, 0, 0))],
        out_specs=pl.BlockSpec((2, 1, 1, ncp, HEAD_DIM), lambda b, g: (0, b, g, 0, 0)),
        out_shape=jax.ShapeDtypeStruct((2, B, KV_GROUPS, ncp, HEAD_DIM), BF16),
        compiler_params=_cparams(("arbitrary", "arbitrary")),
        name="nsa_compress",
    )(xa, xb, pe, w1_kv.astype(BF16), w2_kv.astype(BF16))
    return out.transpose(0, 1, 3, 2, 4).reshape(2, B, ncp, KV_GROUPS * HEAD_DIM)


def _stack_queries(q, tq):
    parts = []
    for h in range(N_STACK):
        q_h = q[:, h * HEAD_DIM:(h + 1) * HEAD_DIM]
        z = jnp.zeros_like(q_h)
        parts.append(jnp.concatenate([q_h, z] if h // REP == 0 else [z, q_h], axis=1))
    return jnp.concatenate(parts, axis=0) * SCALE


def _query_key_offsets(tk, tq):
    key = lax.broadcasted_iota(jnp.int32, (tk, N_STACK * tq), 0)
    qry = lax.broadcasted_iota(jnp.int32, (tk, N_STACK * tq), 1) & (tq - 1)
    return qry - key


def _flash_step(s, m, l, acc, v_t):
    m_new = jnp.maximum(m, jnp.max(s, axis=0, keepdims=True))
    a = jnp.exp(m - m_new)
    p = jnp.exp(s - m_new)
    l_new = a * l + jnp.sum(p, axis=0, keepdims=True)
    return m_new, l_new, a * acc + _dot_tn(v_t, p.astype(BF16))


def _store_heads(acc_n, gates_t, gate_col, o_ref, tq):
    for pair in range(N_STACK // 2):
        g = (2 * pair) // REP
        blocks = []
        for h in (2 * pair, 2 * pair + 1):
            blk = acc_n[g * HEAD_DIM:(g + 1) * HEAD_DIM, h * tq:(h + 1) * tq]
            if gates_t is not None:
                c = h * 3 + gate_col
                blk = blk * gates_t[c:c + 1, :]
            blocks.append(blk)
        o_ref[0, :, pair * LANES:(pair + 1) * LANES] = jnp.concatenate(blocks, axis=0).T


def _head_rows(values, tq):
    return jnp.repeat(jnp.asarray(values, F32), tq).reshape(1, N_STACK * tq)


def _band_kernel(*refs, window, use_sink, gate_col, tq):
    if use_sink:
        slope_ref, sink_ref, q_ref, k_ref, v_ref, o_ref = refs
        g_ref = None
    else:
        slope_ref, q_ref, k_ref, v_ref, g_ref, o_ref = refs
    i = pl.program_id(1)
    n = N_STACK * tq
    n_back = window // tq
    q_stack = _stack_queries(q_ref[0], tq)
    rel = _query_key_offsets(tq, tq)
    slope = slope_ref[...]
    bias0 = slope * rel.astype(F32)
    if use_sink:
        m, l = sink_ref[...], jnp.ones((1, n), F32)
    else:
        m, l = jnp.full((1, n), -jnp.inf, F32), jnp.zeros((1, n), F32)
    acc = jnp.zeros((LANES, n), F32)
    for j in range(n_back + 1):
        start = pl.multiple_of(jnp.maximum(i - j, 0) * tq, tq)
        k_t = k_ref[0, pl.ds(start, tq), :]
        v_t = v_ref[0, pl.ds(start, tq), :]
        s = _dot_nt(k_t, q_stack) - (bias0 + slope * float(j * tq))
        keep = None
        if j == 0:
            keep = rel >= 0
        if j == n_back:
            far = rel < window - j * tq
            keep = far if keep is None else keep & far
        if keep is not None:
            s = jnp.where(keep, s, -jnp.inf)
        if j > 0:
            s = jnp.where(i >= j, s, -jnp.inf)
        m, l, acc = _flash_step(s, m, l, acc, v_t)
    gates_t = None if g_ref is None else jax.nn.sigmoid(g_ref[0].astype(F32)).T
    _store_heads(acc * (1.0 / l), gates_t, gate_col, o_ref, tq)


def _band_attention(proj, qname, kname, vname, window, slopes, sinks, gate_col, B, S):
    tq = ATT_TQ
    n = N_STACK * tq
    kern = functools.partial(_band_kernel, window=window, use_sink=sinks is not None, gate_col=gate_col, tq=tq)
    qb, kb, vb, gbk = SEG_OFF[qname] // 512, SEG_OFF[kname] // 128, SEG_OFF[vname] // 128, SEG_OFF["gb"] // 128
    row = pl.BlockSpec((1, n), lambda b, i: (0, 0))
    specs = [pl.BlockSpec((1, tq, 512), lambda b, i: (b, i, qb)),
             pl.BlockSpec((1, S, 128), lambda b, i: (b, 0, kb)),
             pl.BlockSpec((1, S, 128), lambda b, i: (b, 0, vb))]
    args = [proj, proj, proj]
    if sinks is not None:
        specs = [row, row] + specs
        args = [_head_rows(slopes, tq), _head_rows(sinks, tq)] + args
    else:
        specs = [row] + specs + [pl.BlockSpec((1, tq, 128), lambda b, i: (b, i, gbk))]
        args = [_head_rows(slopes, tq)] + args + [proj]
    return pl.pallas_call(
        kern,
        grid=(B, S // tq),
        in_specs=specs,
        out_specs=pl.BlockSpec((1, tq, 512), lambda b, i: (b, i, 0)),
        out_shape=jax.ShapeDtypeStruct((B, S, 512), F32),
        compiler_params=_cparams(("arbitrary", "arbitrary")),
        name="band_attn_w%d" % window,
    )(*args)


def _cmp_select_kernel(slope_ref, q_ref, kc_ref, vc_ref, g_ref, ovt_ref, o_ref, sel_ref, cnt_ref, *, tq, n_pick):
    i = pl.program_id(1)
    ncp = kc_ref.shape[2]
    n = N_STACK * tq
    q_stack = _stack_queries(q_ref[0], tq)
    t_row = i * tq + (lax.broadcasted_iota(jnp.int32, (1, n), 1) & (tq - 1))
    n_idx = lax.broadcasted_iota(jnp.int32, (ncp, n), 0)
    dist_c = t_row - (n_idx * CMP_STRIDE + (CMP_BLOCK - 1))
    s = _dot_nt(kc_ref[0, 0], q_stack) - slope_ref[...] * dist_c.astype(F32)
    s = jnp.where(dist_c >= 0, s, -jnp.inf)
    m = jnp.max(s, axis=0, keepdims=True)
    m = jnp.where(m == -jnp.inf, 0.0, m)
    e = jnp.exp(s - m)
    den = jnp.sum(e, axis=0, keepdims=True)
    p = e * (1.0 / jnp.maximum(den, 1e-30))
    gates_t = jax.nn.sigmoid(g_ref[0].astype(F32)).T
    _store_heads(_dot_tn(vc_ref[0, 0], p.astype(BF16)), gates_t, 0, o_ref, tq)
    psum = jnp.concatenate(
        [sum(p[:, (g * REP + r) * tq:(g * REP + r + 1) * tq] for r in range(REP)) for g in range(KV_GROUPS)], axis=1)
    p_hi = psum.astype(BF16)
    p_lo = (psum - p_hi.astype(F32)).astype(BF16)
    imp = (jnp.dot(ovt_ref[...], p_hi, preferred_element_type=F32)
           + jnp.dot(ovt_ref[...], p_lo, preferred_element_type=F32))
    w = KV_GROUPS * tq
    j_idx = lax.broadcasted_iota(jnp.int32, (LANES, w), 0)
    t_sel = i * tq + (lax.broadcasted_iota(jnp.int32, (1, w), 1) & (tq - 1))
    cur = t_sel // SEL_BLOCK
    forced = (j_idx == 0) | (j_idx == cur) | (j_idx == cur - 1)
    valid = j_idx * SEL_BLOCK <= t_sel
    imp = jnp.where(valid, jnp.where(forced, FORCE_SCORE, imp), -1.0)

    def pick(_, carry):
        imp, sel = carry
        mx = jnp.max(imp, axis=0, keepdims=True)
        first = jnp.min(jnp.where(imp == mx, j_idx, LANES), axis=0, keepdims=True)
        hit = j_idx == first
        return jnp.where(hit, -2.0, imp), jnp.where(hit, 1.0, sel)

    _, sel = lax.fori_loop(0, n_pick, pick, (imp, jnp.zeros((LANES, w), F32)))
    sel_b = sel.astype(BF16)
    for g in range(KV_GROUPS):
        sel_ref[0, g] = sel_b[:, g * tq:(g + 1) * tq]
    cnt = _dot_nt(jnp.ones((8, w), BF16), sel_b)
    cnt_ref[0, 0] = cnt[0:1, :]


def _overlap_matrix_t(ncp):
    n = np.arange(ncp)[None, :]
    j = np.arange(LANES)[:, None]
    start = n * CMP_STRIDE
    end = start + CMP_BLOCK - 1
    return ((end >= j * SEL_BLOCK) & (start < j * SEL_BLOCK + SEL_BLOCK)).astype(np.float32)


def _cmp_select(proj, kvc, B, S):
    tq = ATT_TQ
    nq = S // tq
    ncp = kvc.shape[2]
    n = N_STACK * tq
    n_pick = min(N_SELECT, S // SEL_BLOCK)
    qb, gbk = SEG_OFF["qb"] // 512, SEG_OFF["gb"] // 128
    ovt = jnp.asarray(_overlap_matrix_t(ncp), BF16)
    kern = functools.partial(_cmp_select_kernel, tq=tq, n_pick=n_pick)
    kv_blk = (1, 1, ncp, LANES)
    return pl.pallas_call(
        kern,
        grid=(B, nq),
        in_specs=[pl.BlockSpec((1, n), lambda b, i: (0, 0)),
                  pl.BlockSpec((1, tq, 512), lambda b, i: (b, i, qb)),
                  pl.BlockSpec(kv_blk, lambda b, i: (0, b, 0, 0)),
                  pl.BlockSpec(kv_blk, lambda b, i: (1, b, 0, 0)),
                  pl.BlockSpec((1, tq, 128), lambda b, i: (b, i, gbk)),
                  pl.BlockSpec((LANES, ncp), lambda b, i: (0, 0))],
        out_specs=[pl.BlockSpec((1, tq, 512), lambda b, i: (b, i, 0)),
                   pl.BlockSpec((1, KV_GROUPS, LANES, tq), lambda b, i: (b, 0, 0, i)),
                   pl.BlockSpec((1, 1, 1, LANES), lambda b, i: (b, i, 0, 0))],
        out_shape=[jax.ShapeDtypeStruct((B, S, 512), F32),
                   jax.ShapeDtypeStruct((B, KV_GROUPS, LANES, S), BF16),
                   jax.ShapeDtypeStruct((B, nq, 1, LANES), F32)],
        compiler_params=_cparams(("arbitrary", "arbitrary")),
        name="nsa_cmp_select",
    )(_head_rows(SLOPES[HA:], tq), proj, kvc, kvc, proj, ovt)


def _sel_kernel(flag_ref, slope_ref, q_ref, k_ref, v_ref, sel_ref, g_ref, o_ref, m_s, l_s, acc_s, *, tq, nq):
    b = pl.program_id(0)
    i = pl.program_id(1)
    q_stack = _stack_queries(q_ref[0], tq)
    rel = _query_key_offsets(tq, tq)
    slope = slope_ref[...]
    bias0 = slope * rel.astype(F32)
    per_tile = tq // SEL_BLOCK
    key_blk = lax.broadcasted_iota(jnp.int32, (tq, LANES), 0) // SEL_BLOCK
    blk_id = lax.broadcasted_iota(jnp.int32, (tq, LANES), 1)
    m_s[...] = jnp.full(m_s.shape, -jnp.inf, F32)
    l_s[...] = jnp.zeros(l_s.shape, F32)
    acc_s[...] = jnp.zeros(acc_s.shape, F32)

    def tile(kt, diagonal):
        start = pl.multiple_of(kt * tq, tq)
        k_t = k_ref[0, pl.ds(start, tq), :]
        v_t = v_ref[0, pl.ds(start, tq), :]
        expand = jnp.where(blk_id == kt * per_tile + key_blk, 1.0, 0.0).astype(BF16)
        picked = [jnp.dot(expand, sel_ref[0, g], preferred_element_type=F32) for g in range(KV_GROUPS)]
        keep = jnp.concatenate([picked[h // REP] for h in range(N_STACK)], axis=1) > 0.5
        if diagonal:
            keep = keep & (rel >= 0)
        s = _dot_nt(k_t, q_stack) - (bias0 + slope * ((i - kt) * tq).astype(F32))
        s = jnp.where(keep, s, -jnp.inf)
        m, l, acc = _flash_step(s, m_s[...], l_s[...], acc_s[...], v_t)
        m_s[...] = m
        l_s[...] = l
        acc_s[...] = acc

    tile(i, True)

    def body(j, _):
        kt = i - 1 - j

        @pl.when(flag_ref[(b * nq + i) * nq + kt] > 0)
        def _():
            tile(kt, False)

        return 0

    lax.fori_loop(0, i, body, 0)
    gates_t = jax.nn.sigmoid(g_ref[0].astype(F32)).T
    _store_heads(acc_s[...] * (1.0 / l_s[...]), gates_t, 1, o_ref, tq)


def _selected_attention(proj, sel, flags, B, S):
    tq = ATT_TQ
    nq = S // tq
    n = N_STACK * tq
    qb, kb, vb, gbk = SEG_OFF["qb"] // 512, SEG_OFF["kbs"] // 128, SEG_OFF["vbs"] // 128, SEG_OFF["gb"] // 128
    kern = functools.partial(_sel_kernel, tq=tq, nq=nq)
    grid_spec = pltpu.PrefetchScalarGridSpec(
        num_scalar_prefetch=1,
        grid=(B, nq),
        in_specs=[pl.BlockSpec((1, n), lambda b, i, f: (0, 0)),
                  pl.BlockSpec((1, tq, 512), lambda b, i, f: (b, i, qb)),
                  pl.BlockSpec((1, S, 128), lambda b, i, f: (b, 0, kb)),
                  pl.BlockSpec((1, S, 128), lambda b, i, f: (b, 0, vb)),
                  pl.BlockSpec((1, KV_GROUPS, LANES, tq), lambda b, i, f: (b, 0, 0, i)),
                  pl.BlockSpec((1, tq, 128), lambda b, i, f: (b, i, gbk))],
        out_specs=pl.BlockSpec((1, tq, 512), lambda b, i, f: (b, i, 0)),
        scratch_shapes=[pltpu.VMEM((1, n), F32), pltpu.VMEM((1, n), F32), pltpu.VMEM((LANES, n), F32)],
    )
    return pl.pallas_call(
        kern,
        grid_spec=grid_spec,
        out_shape=jax.ShapeDtypeStruct((B, S, 512), F32),
        compiler_params=_cparams(("arbitrary", "arbitrary")),
        name="nsa_selected",
    )(flags, _head_rows(SLOPES[HA:], tq), proj, proj, proj, sel, proj)


def _selection_flags(cnt, B, S):
    nq = S // ATT_TQ
    per_tile = ATT_TQ // SEL_BLOCK
    c = cnt[:, :, 0, :nq * per_tile].reshape(B, nq, nq, per_tile).sum(-1)
    return (c > 0).astype(jnp.int32).reshape(-1)


def _stick_kernel(q_ref, k_ref, v_ref, o_ref, *, t):
    i = pl.program_id(2)
    n = 2 * t
    lane = lax.broadcasted_iota(jnp.int32, (t, LANES), 1)
    q = q_ref[0]
    zero = jnp.zeros_like(q)
    q_stack = jnp.concatenate([jnp.where(lane < HEAD_DIM, q, zero), jnp.where(lane >= HEAD_DIM, q, zero)], axis=0) * SCALE
    key = lax.broadcasted_iota(jnp.int32, (t, n), 0)
    qry = lax.broadcasted_iota(jnp.int32, (t, n), 1) & (t - 1)
    causal = key < qry
    r_i = lax.broadcasted_iota(jnp.int32, (t, t), 0)
    c_i = lax.broadcasted_iota(jnp.int32, (t, t), 1)
    suffix = jnp.where(c_i >= r_i, 1.0, 0.0).astype(BF16)

    def tile(kt, carry, acc, masked):
        start = pl.multiple_of(kt * t, t)
        k_t = k_ref[0, pl.ds(start, t), :]
        v_t = v_ref[0, pl.ds(start, t), :]
        z = _dot_nt(k_t, q_stack)
        log_1mb = -(jnp.maximum(z, 0.0) + jnp.log(1.0 + jnp.exp(-jnp.abs(z))))
        if masked:
            log_1mb = jnp.where(causal, log_1mb, 0.0)
        hi = log_1mb.astype(BF16)
        lo = (log_1mb - hi.astype(F32)).astype(BF16)
        inc = (jnp.dot(suffix, hi, preferred_element_type=F32)
               + jnp.dot(suffix, lo, preferred_element_type=F32))
        a = jnp.exp(z + inc + carry)
        if masked:
            a = jnp.where(causal, a, 0.0)
        return carry + inc[0:1, :], acc + _dot_tn(v_t, a.astype(BF16))

    carry, acc = tile(i, jnp.zeros((1, n), F32), jnp.zeros((LANES, n), F32), True)

    def cond(state):
        kt, go, _, _ = state
        return (kt >= 0) & (go > 0)

    def body(state):
        kt, _, carry, acc = state
        carry, acc = tile(kt, carry, acc, False)
        return kt - 1, (jnp.max(carry) > -STICK_EXIT).astype(jnp.int32), carry, acc

    go0 = (jnp.max(carry) > -STICK_EXIT).astype(jnp.int32)
    _, _, _, acc = lax.while_loop(cond, body, (i - 1, go0, carry, acc))
    o_ref[0] = jnp.concatenate([acc[:HEAD_DIM, :t], acc[HEAD_DIM:, t:]], axis=0).T


def _stick_attention(proj, B, S):
    t = min(STICK_T, S)
    qb, kb, vb = SEG_OFF["qc"] // 128, SEG_OFF["kc"] // 128, SEG_OFF["vc"] // 128
    npair = HC // 2
    return pl.pallas_call(
        functools.partial(_stick_kernel, t=t),
        grid=(B, npair, S // t),
        in_specs=[pl.BlockSpec((1, t, 128), lambda b, p, i: (b, i, qb + p)),
                  pl.BlockSpec((1, S, 128), lambda b, p, i: (b, 0, kb + p)),
                  pl.BlockSpec((1, S, 128), lambda b, p, i: (b, 0, vb + p))],
        out_specs=pl.BlockSpec((1, t, 128), lambda b, p, i: (b, i, p)),
        out_shape=jax.ShapeDtypeStruct((B, S, HC * HEAD_DIM), F32),
        compiler_params=_cparams(("arbitrary", "arbitrary", "arbitrary")),
        name="stick_breaking",
    )(proj, proj, proj)


def _group_rms(o, eps=1e-6):
    return o * lax.rsqrt(jnp.mean(o * o, axis=-1, keepdims=True) + eps)


def _outproj_kernel(oa_ref, ob1_ref, ob2_ref, ob3_ref, oc_ref, x_ref, gain_ref, w_ref, g1_ref, lg_ref, lb_ref,
                    sc2_ref, sh2_ref, wr_ref, br_ref, x1_ref, h2_ref, route_ref):
    o_b = ob1_ref[0] + ob2_ref[0] + ob3_ref[0]
    merged = jnp.concatenate([_group_rms(oa_ref[0]), _group_rms(o_b), _group_rms(oc_ref[0])], axis=-1)
    merged = (merged * gain_ref[...]).astype(BF16)
    y = jnp.dot(merged, w_ref[...], preferred_element_type=F32)
    x1 = _layer_norm(ALPHA * x_ref[0] + (1.0 + g1_ref[0]) * y) * lg_ref[...] + lb_ref[...]
    x1_ref[0] = x1
    h2 = _layer_norm(x1) * (1.0 + sc2_ref[0]) + sh2_ref[0]
    h2_ref[0] = h2.astype(BF16)
    logits = jnp.dot(h2, wr_ref[...], preferred_element_type=F32, precision=HIGHEST) + br_ref[...]
    tm = logits.shape[0]
    lane = lax.broadcasted_iota(jnp.int32, (tm, LANES), 1)
    lg = jnp.where(lane < N_GROUPS, logits, -jnp.inf)
    mg = jnp.max(lg, axis=-1, keepdims=True)
    p_group = 1.0 / jnp.sum(jnp.exp(lg - mg), axis=-1, keepdims=True)
    g_sel = jnp.min(jnp.where(lg == mg, lane, LANES), axis=-1, keepdims=True)
    e_idx = lane - N_GROUPS
    in_grp = (e_idx >= g_sel * EXPERTS_PER_GROUP) & (e_idx < (g_sel + 1) * EXPERTS_PER_GROUP)
    le = jnp.where(in_grp, logits, -jnp.inf)
    m1 = jnp.max(le, axis=-1, keepdims=True)
    i1 = jnp.min(jnp.where(le == m1, e_idx, LANES), axis=-1, keepdims=True)
    le2 = jnp.where(e_idx == i1, -jnp.inf, le)
    m2 = jnp.max(le2, axis=-1, keepdims=True)
    i2 = jnp.min(jnp.where(le2 == m2, e_idx, LANES), axis=-1, keepdims=True)
    den = jnp.sum(jnp.exp(le - m1), axis=-1, keepdims=True)
    p1 = 1.0 / den
    p2 = jnp.exp(m2 - m1) / den
    gate1 = p_group * p1 / (p1 + p2)
    gate2 = p_group * p2 / (p1 + p2)
    route = jnp.where(lane == 0, i1.astype(F32),
                      jnp.where(lane == 1, i2.astype(F32),
                                jnp.where(lane == 2, gate1, jnp.where(lane == 3, gate2, 0.0))))
    route_ref[0] = route


def _out_projection(o_a, o_b1, o_b2, o_b3, o_c, x, gain, w_out, g1, ln_g, ln_b, sc2, sh2, w_r, b_r, B, S):
    D = D_MODEL
    tm = 256
    row = lambda b, i: (b, i, 0)
    vec = lambda b, i: (0, 0)
    bvec = lambda b, i: (b, 0, 0)
    return pl.pallas_call(
        _outproj_kernel,
        grid=(B, S // tm),
        in_specs=[pl.BlockSpec((1, tm, 512), row), pl.BlockSpec((1, tm, 512), row),
                  pl.BlockSpec((1, tm, 512), row), pl.BlockSpec((1, tm, 512), row),
                  pl.BlockSpec((1, tm, 1024), row), pl.BlockSpec((1, tm, D), row),
                  pl.BlockSpec((1, D), vec), pl.BlockSpec((D, D), vec),
                  pl.BlockSpec((1, 1, D), bvec), pl.BlockSpec((1, D), vec), pl.BlockSpec((1, D), vec),
                  pl.BlockSpec((1, 1, D), bvec), pl.BlockSpec((1, 1, D), bvec),
                  pl.BlockSpec((D, LANES), vec), pl.BlockSpec((1, LANES), vec)],
        out_specs=[pl.BlockSpec((1, tm, D), row), pl.BlockSpec((1, tm, D), row),
                   pl.BlockSpec((1, tm, LANES), row)],
        out_shape=[jax.ShapeDtypeStruct((B, S, D), F32), jax.ShapeDtypeStruct((B, S, D), BF16),
                   jax.ShapeDtypeStruct((B, S, LANES), F32)],
        compiler_params=_cparams(("arbitrary", "arbitrary")),
        name="out_proj_norm_route",
    )(o_a, o_b1, o_b2, o_b3, o_c, x, gain, w_out, g1, ln_g, ln_b, sc2, sh2, w_r, b_r)


def _expert_kernel(be_ref, nb_ref, x_ref, gw_ref, wg_ref, wu_ref, wd_ref, y_ref, wg_s, wu_s, wd_s):
    i = pl.program_id(0)

    @pl.when(i < nb_ref[0])
    def _():
        @pl.when((i == 0) | (be_ref[i] != be_ref[jnp.maximum(i - 1, 0)]))
        def _():
            wg_s[...] = wg_ref[0].astype(BF16)
            wu_s[...] = wu_ref[0].astype(BF16)
            wd_s[...] = wd_ref[0].astype(BF16)

        x = x_ref[...]
        hg = jnp.dot(x, wg_s[...], preferred_element_type=F32)
        hu = jnp.dot(x, wu_s[...], preferred_element_type=F32)
        hid = (hg * jax.nn.sigmoid(hg)) * hu
        y = jnp.dot(hid.astype(BF16), wd_s[...], preferred_element_type=F32)
        y_ref[...] = y * gw_ref[...]

    @pl.when(i >= nb_ref[0])
    def _():
        y_ref[...] = jnp.zeros(y_ref.shape, F32)


def _expert_mlp(blk_expert, n_used, xs, row_gate, w_gate, w_up, w_down, bm):
    rows, D = xs.shape
    nblk = rows // bm
    grid_spec = pltpu.PrefetchScalarGridSpec(
        num_scalar_prefetch=2,
        grid=(nblk,),
        in_specs=[pl.BlockSpec((bm, D), lambda i, be, nb: (i, 0)),
                  pl.BlockSpec((bm, 1), lambda i, be, nb: (i, 0)),
                  pl.BlockSpec((1, D, D_EXPERT), lambda i, be, nb: (be[i], 0, 0)),
                  pl.BlockSpec((1, D, D_EXPERT), lambda i, be, nb: (be[i], 0, 0)),
                  pl.BlockSpec((1, D_EXPERT, D), lambda i, be, nb: (be[i], 0, 0))],
        out_specs=pl.BlockSpec((bm, D), lambda i, be, nb: (i, 0)),
        scratch_shapes=[pltpu.VMEM((D, D_EXPERT), BF16), pltpu.VMEM((D, D_EXPERT), BF16),
                        pltpu.VMEM((D_EXPERT, D), BF16)],
    )
    return pl.pallas_call(
        _expert_kernel,
        grid_spec=grid_spec,
        out_shape=jax.ShapeDtypeStruct((rows, D), F32),
        compiler_params=_cparams(("arbitrary",)),
        name="expert_mlp",
    )(blk_expert, n_used, xs, row_gate, w_gate, w_up, w_down)


def _postnorm_kernel(x_ref, y_ref, g2_ref, lg_ref, lb_ref, o_ref):
    o_ref[0] = _layer_norm(ALPHA * x_ref[0] + (1.0 + g2_ref[0]) * y_ref[0]) * lg_ref[...] + lb_ref[...]


def _post_norm(x1, y, g2, ln_g, ln_b, B, S):
    D = D_MODEL
    tm = 512 if S % 512 == 0 else S
    row = lambda b, i: (b, i, 0)
    return pl.pallas_call(
        _postnorm_kernel,
        grid=(B, S // tm),
        in_specs=[pl.BlockSpec((1, tm, D), row), pl.BlockSpec((1, tm, D), row),
                  pl.BlockSpec((1, 1, D), lambda b, i: (b, 0, 0)),
                  pl.BlockSpec((1, D), lambda b, i: (0, 0)), pl.BlockSpec((1, D), lambda b, i: (0, 0))],
        out_specs=pl.BlockSpec((1, tm, D), row),
        out_shape=jax.ShapeDtypeStruct((B, S, D), F32),
        compiler_params=_cparams(("arbitrary", "arbitrary")),
        name="moe_post_norm",
    )(x1, y, g2, ln_g, ln_b)


MOE_BM = 256


def _moe(h2, route, w_gate, w_up, w_down, B, S):
    D = D_MODEL
    N = B * S
    A = 2 * N
    bm = MOE_BM
    r = route.reshape(N, LANES)
    e_flat = r[:, 0:2].astype(jnp.int32).reshape(A)
    w_flat = r[:, 2:4].reshape(A)
    onehot = (e_flat[:, None] == jnp.arange(N_EXPERTS)[None, :]).astype(jnp.int32)
    csum = jnp.cumsum(onehot, axis=0)
    counts = csum[-1]
    rank = jnp.sum((csum - onehot) * onehot, axis=1)
    padded = (counts + bm - 1) // bm * bm
    pad_end = jnp.cumsum(padded)
    pad_start = pad_end - padded
    dest = pad_start[e_flat] + rank
    nblk = -(-A // bm) + N_EXPERTS
    rows = nblk * bm
    row_tok = jnp.zeros((rows,), jnp.int32).at[dest].set(jnp.arange(A, dtype=jnp.int32) // 2)
    row_gate = jnp.zeros((rows,), F32).at[dest].set(w_flat)
    blk_start = jnp.arange(nblk) * bm
    blk_expert = jnp.minimum(jnp.sum(blk_start[:, None] >= pad_end[None, :], axis=1), N_EXPERTS - 1).astype(jnp.int32)
    n_used = (pad_end[-1] // bm).astype(jnp.int32).reshape(1)
    xs = h2.reshape(N, D)[row_tok]
    yb = _expert_mlp(blk_expert, n_used, xs, row_gate.reshape(rows, 1), w_gate, w_up, w_down, bm)
    d2 = dest.reshape(N, 2)
    y = yb[d2[:, 0]] + yb[d2[:, 1]]
    return y.reshape(B, S, D)


def kernel(x, c, w_ada, b_ada, w_in, sinks, cmp_pe_k, cmp_w1_k, cmp_w2_k, cmp_pe_v, cmp_w1_v, cmp_w2_v, mix_gain, w_out, ln1_g, ln1_b, w_rg, b_rg, w_re, b_re, w_gate, w_up, w_down, ln2_g, ln2_b):
    B, S, D = x.shape
    L = w_ada.shape[0]
    mod = _ada_mod(c, w_ada, b_ada)
    for l in range(L):
        sh1, sc1, g1, sh2, sc2, g2 = [mod[l, :, k * D:(k + 1) * D].reshape(B, 1, D) for k in range(6)]
        proj = _in_projection(x.reshape(B * S, D), sc1, sh1, _pack_w_in(w_in[l]), S).reshape(B, S, D_PROJ)
        o_a = _band_attention(proj, "qa", "ka", "va", WINDOW_A, SLOPES[:HA], sinks[l], 0, B, S)
        kvc = _compress(proj, jnp.stack([cmp_pe_k[l], cmp_pe_v[l]]), jnp.stack([cmp_w1_k[l], cmp_w1_v[l]]),
                        jnp.stack([cmp_w2_k[l], cmp_w2_v[l]]), B, S)
        o_cmp, sel, cnt = _cmp_select(proj, kvc, B, S)
        o_slc = _selected_attention(proj, sel, _selection_flags(cnt, B, S), B, S)
        o_win = _band_attention(proj, "qb", "kbw", "vbw", WINDOW_B, SLOPES[HA:], None, 2, B, S)
        o_c = _stick_attention(proj, B, S)
        w_r = jnp.concatenate([w_rg[l], w_re[l].transpose(1, 0, 2).reshape(D, N_EXPERTS),
                               jnp.zeros((D, LANES - N_GROUPS - N_EXPERTS), F32)], axis=1)
        b_r = jnp.concatenate([b_rg[l], b_re[l].reshape(N_EXPERTS),
                               jnp.zeros((LANES - N_GROUPS - N_EXPERTS,), F32)]).reshape(1, LANES)
        x1, h2, route = _out_projection(o_a, o_cmp, o_slc, o_win, o_c, x, mix_gain[l].reshape(1, D),
                                        w_out[l].astype(BF16), g1, ln1_g[l].reshape(1, D), ln1_b[l].reshape(1, D),
                                        sc2, sh2, w_r, b_r, B, S)
        y = _moe(h2, route, w_gate[l], w_up[l], w_down[l], B, S)
        x = _post_norm(x1, y, g2, ln2_g[l].reshape(1, D), ln2_b[l].reshape(1, D), B, S)
    return x
```

```python
import functools

import numpy as np
import jax
import jax.numpy as jnp
from jax import lax
from jax.experimental import pallas as pl
from jax.experimental.pallas import tpu as pltpu

F32 = jnp.float32
BF16 = jnp.bfloat16
HIGHEST = lax.Precision.HIGHEST

D_MODEL = 2048
DEPTH = 2
HEAD_DIM = 64
HA = 8
HB = 8
HC = 16
KV_GROUPS = 2
REP = 4
N_STACK = KV_GROUPS * REP
SCALE = HEAD_DIM ** -0.5
WINDOW_A = 128
WINDOW_B = 512
CMP_BLOCK = 32
CMP_STRIDE = 16
CMP_HIDDEN = 128
SEL_BLOCK = 64
N_SELECT = 16
FORCE_SCORE = 1e4
N_GROUPS = 4
EXPERTS_PER_GROUP = 8
N_EXPERTS = 32
D_EXPERT = 512
ALPHA = (2.0 * DEPTH) ** 0.25
LANES = 128
VMEM_LIMIT = 48 * 1024 * 1024

SEG_SIZES = dict(qa=512, qb=512, ka=128, va=128, kbc=128, vbc=128, kbs=128, vbs=128,
                 kbw=128, vbw=128, gb=128, qc=1024, kc=1024, vc=1024)
SEG_ORDER = ("qa", "qb", "ka", "va", "kbc", "vbc", "kbs", "vbs", "kbw", "vbw", "gb", "qc", "kc", "vc")
SEG_OFF = {}
_o = 0
for _n in SEG_ORDER:
    SEG_OFF[_n] = _o
    _o += SEG_SIZES[_n]
PROJ_TN = 768
D_PROJ = -(-_o // PROJ_TN) * PROJ_TN
REF_SEGS = (("qa", 512), ("ka", 128), ("va", 128), ("qb", 512), ("kbc", 128), ("vbc", 128), ("kbs", 128),
            ("vbs", 128), ("kbw", 128), ("vbw", 128), ("gb", 24), ("qc", 1024), ("kc", 1024), ("vc", 1024))

SLOPES = [2.0 ** (-8.0 * (i + 1) / (HA + HB)) for i in range(HA + HB)]
STICK_EXIT = 105.0

ATT_TQ = 128
STICK_T = 256


def _cparams(sem):
    return pltpu.CompilerParams(dimension_semantics=sem, vmem_limit_bytes=VMEM_LIMIT)


def _dot_nt(a, b):
    return lax.dot_general(a, b, (((1,), (1,)), ((), ())), preferred_element_type=F32)


def _dot_tn(a, b):
    return lax.dot_general(a, b, (((0,), (0,)), ((), ())), preferred_element_type=F32)


def _layer_norm(x, eps=1e-5):
    mu = jnp.mean(x, axis=-1, keepdims=True)
    xc = x - mu
    var = jnp.mean(xc * xc, axis=-1, keepdims=True)
    return xc * lax.rsqrt(var + eps)


def _ada_kernel(c_ref, w_ref, b_ref, o_ref):
    c = c_ref[...]
    cs = c * jax.nn.sigmoid(c)
    o_ref[0] = jnp.dot(cs, w_ref[0], preferred_element_type=F32, precision=HIGHEST) + b_ref[0]


def _ada_mod(c, w_ada, b_ada):
    B = c.shape[0]
    L, D, N6 = w_ada.shape
    cp = jnp.zeros((8, D), F32).at[:B].set(c)
    tn = 1024
    out = pl.pallas_call(
        _ada_kernel,
        grid=(L, N6 // tn),
        in_specs=[pl.BlockSpec((8, D), lambda l, j: (0, 0)),
                  pl.BlockSpec((1, D, tn), lambda l, j: (l, 0, j)),
                  pl.BlockSpec((1, 1, tn), lambda l, j: (l, 0, j))],
        out_specs=pl.BlockSpec((1, 8, tn), lambda l, j: (l, 0, j)),
        out_shape=jax.ShapeDtypeStruct((L, 8, N6), F32),
        compiler_params=_cparams(("arbitrary", "arbitrary")),
        name="ada_mod",
    )(cp, w_ada, b_ada.reshape(L, 1, N6))
    return out[:, :B]


def _inproj_kernel(x_ref, sc_ref, sh_ref, w_ref, o_ref, h_ref):
    @pl.when(pl.program_id(1) == 0)
    def _():
        h = _layer_norm(x_ref[...]) * (1.0 + sc_ref[0]) + sh_ref[0]
        h_ref[...] = h.astype(BF16)

    o_ref[...] = jnp.dot(h_ref[...], w_ref[...], preferred_element_type=F32).astype(BF16)


def _in_projection(x2d, sc, sh, w_packed, S):
    N, D = x2d.shape
    tm = 1024 if S % 1024 == 0 else S
    return pl.pallas_call(
        _inproj_kernel,
        grid=(N // tm, D_PROJ // PROJ_TN),
        in_specs=[pl.BlockSpec((tm, D), lambda i, j: (i, 0)),
                  pl.BlockSpec((1, 1, D), lambda i, j: (i * tm // S, 0, 0)),
                  pl.BlockSpec((1, 1, D), lambda i, j: (i * tm // S, 0, 0)),
                  pl.BlockSpec((D, PROJ_TN), lambda i, j: (0, j))],
        out_specs=pl.BlockSpec((tm, PROJ_TN), lambda i, j: (i, j)),
        out_shape=jax.ShapeDtypeStruct((N, D_PROJ), BF16),
        scratch_shapes=[pltpu.VMEM((tm, D), BF16)],
        compiler_params=_cparams(("arbitrary", "arbitrary")),
        name="ln_in_proj",
    )(x2d, sc, sh, w_packed)


def _pack_w_in(w_in_l):
    cols = {}
    off = 0
    for name, width in REF_SEGS:
        cols[name] = w_in_l[:, off:off + width]
        off += width
    parts = []
    for name in SEG_ORDER:
        wseg = cols[name]
        pad = SEG_SIZES[name] - wseg.shape[1]
        if pad:
            wseg = jnp.pad(wseg, ((0, 0), (0, pad)))
        parts.append(wseg)
    total = sum(SEG_SIZES.values())
    parts.append(jnp.zeros((w_in_l.shape[0], D_PROJ - total), w_in_l.dtype))
    return jnp.concatenate(parts, axis=1).astype(BF16)


def _compress_kernel(xa_ref, xb_ref, pe_ref, w1_ref, w2_ref, o_ref):
    half = CMP_STRIDE * HEAD_DIM
    for kv in range(2):
        xa = (xa_ref[kv, 0, 0].astype(F32) + pe_ref[kv, :, :half]).astype(BF16)
        xb = (xb_ref[kv, 0, 0].astype(F32) + pe_ref[kv, :, half:]).astype(BF16)
        hid = (jnp.dot(xa, w1_ref[kv, :half, :], preferred_element_type=F32)
               + jnp.dot(xb, w1_ref[kv, half:, :], preferred_element_type=F32))
        act = jax.nn.gelu(hid)
        o_ref[kv, 0, 0] = jnp.dot(act.astype(BF16), w2_ref[kv], preferred_element_type=F32).astype(BF16)


def _compress(proj, pe_kv, w1_kv, w2_kv, B, S):
    nchunk = S // CMP_STRIDE
    ncp = nchunk

    def chunks(name):
        t = proj[:, :, SEG_OFF[name]:SEG_OFF[name] + 128].reshape(B, nchunk, CMP_STRIDE, KV_GROUPS, HEAD_DIM)
        return t.transpose(0, 3, 1, 2, 4).reshape(B, KV_GROUPS, nchunk, CMP_STRIDE * HEAD_DIM)

    x16 = jnp.stack([chunks("kbc"), chunks("vbc")])
    xa = x16
    xb = jnp.concatenate([x16[:, :, :, 1:], jnp.zeros_like(x16[:, :, :, :1])], axis=3)
    pe = pe_kv.reshape(2, 1, CMP_BLOCK * HEAD_DIM)
    blk = (2, 1, 1, ncp, CMP_STRIDE * HEAD_DIM)
    out = pl.pallas_call(
        _compress_kernel,
        grid=(B, KV_GROUPS),
        in_specs=[pl.BlockSpec(blk, lambda b, g: (0, b, g, 0, 0)),
                  pl.BlockSpec(blk, lambda b, g: (0, b, g, 0, 0)),
                  pl.BlockSpec((2, 1, CMP_BLOCK * HEAD_DIM), lambda b, g: (0, 0, 0)),
                  pl.BlockSpec((2, CMP_BLOCK * HEAD_DIM, CMP_HIDDEN), lambda b, g: (0, 0, 0)),
                  pl.BlockSpec((2, CMP_HIDDEN, HEAD_DIM), lambda b, g: (0, 0, 0))],
        out_specs=pl.BlockSpec((2, 1, 1, ncp, HEAD_DIM), lambda b, g: (0, b, g, 0, 0)),
        out_shape=jax.ShapeDtypeStruct((2, B, KV_GROUPS, ncp, HEAD_DIM), BF16),
        compiler_params=_cparams(("arbitrary", "arbitrary")),
        name="nsa_compress",
    )(xa, xb, pe, w1_kv.astype(BF16), w2_kv.astype(BF16))
    return out.transpose(0, 1, 3, 2, 4).reshape(2, B, ncp, KV_GROUPS * HEAD_DIM)


def _stack_queries(q, tq):
    parts = []
    for h in range(N_STACK):
        q_h = q[:, h * HEAD_DIM:(h + 1) * HEAD_DIM]
        z = jnp.zeros_like(q_h)
        parts.append(jnp.concatenate([q_h, z] if h // REP == 0 else [z, q_h], axis=1))
    return jnp.concatenate(parts, axis=0) * SCALE


def _query_key_offsets(tk, tq):
    key = lax.broadcasted_iota(jnp.int32, (tk, N_STACK * tq), 0)
    qry = lax.broadcasted_iota(jnp.int32, (tk, N_STACK * tq), 1) & (tq - 1)
    return qry - key


def _flash_step(s, m, l, acc, v_t):
    m_new = jnp.maximum(m, jnp.max(s, axis=0, keepdims=True))
    a = jnp.exp(m - m_new)
    p = jnp.exp(s - m_new)
    l_new = a * l + jnp.sum(p, axis=0, keepdims=True)
    return m_new, l_new, a * acc + _dot_tn(v_t, p.astype(BF16))


def _store_heads(acc_n, gates_t, gate_col, o_ref, tq):
    for pair in range(N_STACK // 2):
        g = (2 * pair) // REP
        blocks = []
        for h in (2 * pair, 2 * pair + 1):
            blk = acc_n[g * HEAD_DIM:(g + 1) * HEAD_DIM, h * tq:(h + 1) * tq]
            if gates_t is not None:
                c = h * 3 + gate_col
                blk = blk * gates_t[c:c + 1, :]
            blocks.append(blk)
        o_ref[0, :, pair * LANES:(pair + 1) * LANES] = jnp.concatenate(blocks, axis=0).T


def _head_rows(values, tq):
    return jnp.repeat(jnp.asarray(values, F32), tq).reshape(1, N_STACK * tq)


def _band_kernel(*refs, window, use_sink, gate_col, tq):
    if use_sink:
        slope_ref, sink_ref, q_ref, k_ref, v_ref, o_ref = refs
        g_ref = None
    else:
        slope_ref, q_ref, k_ref, v_ref, g_ref, o_ref = refs
    i = pl.program_id(1)
    n = N_STACK * tq
    n_back = window // tq
    q_stack = _stack_queries(q_ref[0], tq)
    rel = _query_key_offsets(tq, tq)
    slope = slope_ref[...]
    bias0 = slope * rel.astype(F32)
    if use_sink:
        m, l = sink_ref[...], jnp.ones((1, n), F32)
    else:
        m, l = jnp.full((1, n), -jnp.inf, F32), jnp.zeros((1, n), F32)
    acc = jnp.zeros((LANES, n), F32)
    for j in range(n_back + 1):
        start = pl.multiple_of(jnp.maximum(i - j, 0) * tq, tq)
        k_t = k_ref[0, pl.ds(start, tq), :]
        v_t = v_ref[0, pl.ds(start, tq), :]
        s = _dot_nt(k_t, q_stack) - (bias0 + slope * float(j * tq))
        keep = None
        if j == 0:
            keep = rel >= 0
        if j == n_back:
            far = rel < window - j * tq
            keep = far if keep is None else keep & far
        if keep is not None:
            s = jnp.where(keep, s, -jnp.inf)
        if j > 0:
            s = jnp.where(i >= j, s, -jnp.inf)
        m, l, acc = _flash_step(s, m, l, acc, v_t)
    gates_t = None if g_ref is None else jax.nn.sigmoid(g_ref[0].astype(F32)).T
    _store_heads(acc * (1.0 / l), gates_t, gate_col, o_ref, tq)


def _band_attention(proj, qname, kname, vname, window, slopes, sinks, gate_col, B, S):
    tq = ATT_TQ
    n = N_STACK * tq
    kern = functools.partial(_band_kernel, window=window, use_sink=sinks is not None, gate_col=gate_col, tq=tq)
    qb, kb, vb, gbk = SEG_OFF[qname] // 512, SEG_OFF[kname] // 128, SEG_OFF[vname] // 128, SEG_OFF["gb"] // 128
    row = pl.BlockSpec((1, n), lambda b, i: (0, 0))
    specs = [pl.BlockSpec((1, tq, 512), lambda b, i: (b, i, qb)),
             pl.BlockSpec((1, S, 128), lambda b, i: (b, 0, kb)),
             pl.BlockSpec((1, S, 128), lambda b, i: (b, 0, vb))]
    args = [proj, proj, proj]
    if sinks is not None:
        specs = [row, row] + specs
        args = [_head_rows(slopes, tq), _head_rows(sinks, tq)] + args
    else:
        specs = [row] + specs + [pl.BlockSpec((1, tq, 128), lambda b, i: (b, i, gbk))]
        args = [_head_rows(slopes, tq)] + args + [proj]
    return pl.pallas_call(
        kern,
        grid=(B, S // tq),
        in_specs=specs,
        out_specs=pl.BlockSpec((1, tq, 512), lambda b, i: (b, i, 0)),
        out_shape=jax.ShapeDtypeStruct((B, S, 512), F32),
        compiler_params=_cparams(("arbitrary", "arbitrary")),
        name="band_attn_w%d" % window,
    )(*args)


def _cmp_select_kernel(slope_ref, q_ref, kc_ref, vc_ref, g_ref, ovt_ref, o_ref, sel_ref, cnt_ref, *, tq, n_pick):
    i = pl.program_id(1)
    ncp = kc_ref.shape[2]
    n = N_STACK * tq
    q_stack = _stack_queries(q_ref[0], tq)
    t_row = i * tq + (lax.broadcasted_iota(jnp.int32, (1, n), 1) & (tq - 1))
    n_idx = lax.broadcasted_iota(jnp.int32, (ncp, n), 0)
    dist_c = t_row - (n_idx * CMP_STRIDE + (CMP_BLOCK - 1))
    s = _dot_nt(kc_ref[0, 0], q_stack) - slope_ref[...] * dist_c.astype(F32)
    s = jnp.where(dist_c >= 0, s, -jnp.inf)
    m = jnp.max(s, axis=0, keepdims=True)
    m = jnp.where(m == -jnp.inf, 0.0, m)
    e = jnp.exp(s - m)
    den = jnp.sum(e, axis=0, keepdims=True)
    p = e * (1.0 / jnp.maximum(den, 1e-30))
    gates_t = jax.nn.sigmoid(g_ref[0].astype(F32)).T
    _store_heads(_dot_tn(vc_ref[0, 0], p.astype(BF16)), gates_t, 0, o_ref, tq)
    psum = jnp.concatenate(
        [sum(p[:, (g * REP + r) * tq:(g * REP + r + 1) * tq] for r in range(REP)) for g in range(KV_GROUPS)], axis=1)
    p_hi = psum.astype(BF16)
    p_lo = (psum - p_hi.astype(F32)).astype(BF16)
    imp = (jnp.dot(ovt_ref[...], p_hi, preferred_element_type=F32)
           + jnp.dot(ovt_ref[...], p_lo, preferred_element_type=F32))
    w = KV_GROUPS * tq
    j_idx = lax.broadcasted_iota(jnp.int32, (LANES, w), 0)
    t_sel = i * tq + (lax.broadcasted_iota(jnp.int32, (1, w), 1) & (tq - 1))
    cur = t_sel // SEL_BLOCK
    forced = (j_idx == 0) | (j_idx == cur) | (j_idx == cur - 1)
    valid = j_idx * SEL_BLOCK <= t_sel
    imp = jnp.where(valid, jnp.where(forced, FORCE_SCORE, imp), -1.0)

    def pick(_, carry):
        imp, sel = carry
        mx = jnp.max(imp, axis=0, keepdims=True)
        first = jnp.min(jnp.where(imp == mx, j_idx, LANES), axis=0, keepdims=True)
        hit = j_idx == first
        return jnp.where(hit, -2.0, imp), jnp.where(hit, 1.0, sel)

    _, sel = lax.fori_loop(0, n_pick, pick, (imp, jnp.zeros((LANES, w), F32)))
    sel_b = sel.astype(BF16)
    for g in range(KV_GROUPS):
        sel_ref[0, g] = sel_b[:, g * tq:(g + 1) * tq]
    cnt = _dot_nt(jnp.ones((8, w), BF16), sel_b)
    cnt_ref[0, 0] = cnt[0:1, :]


def _overlap_matrix_t(ncp):
    n = np.arange(ncp)[None, :]
    j = np.arange(LANES)[:, None]
    start = n * CMP_STRIDE
    end = start + CMP_BLOCK - 1
    return ((end >= j * SEL_BLOCK) & (start < j * SEL_BLOCK + SEL_BLOCK)).astype(np.float32)


def _cmp_select(proj, kvc, B, S):
    tq = ATT_TQ
    nq = S // tq
    ncp = kvc.shape[2]
    n = N_STACK * tq
    n_pick = min(N_SELECT, S // SEL_BLOCK)
    qb, gbk = SEG_OFF["qb"] // 512, SEG_OFF["gb"] // 128
    ovt = jnp.asarray(_overlap_matrix_t(ncp), BF16)
    kern = functools.partial(_cmp_select_kernel, tq=tq, n_pick=n_pick)
    kv_blk = (1, 1, ncp, LANES)
    return pl.pallas_call(
        kern,
        grid=(B, nq),
        in_specs=[pl.BlockSpec((1, n), lambda b, i: (0, 0)),
                  pl.BlockSpec((1, tq, 512), lambda b, i: (b, i, qb)),
                  pl.BlockSpec(kv_blk, lambda b, i: (0, b, 0, 0)),
                  pl.BlockSpec(kv_blk, lambda b, i: (1, b, 0, 0)),
                  pl.BlockSpec((1, tq, 128), lambda b, i: (b, i, gbk)),
                  pl.BlockSpec((LANES, ncp), lambda b, i: (0, 0))],
        out_specs=[pl.BlockSpec((1, tq, 512), lambda b, i: (b, i, 0)),
                   pl.BlockSpec((1, KV_GROUPS, LANES, tq), lambda b, i: (b, 0, 0, i)),
                   pl.BlockSpec((1, 1, 1, LANES), lambda b, i: (b, i, 0, 0))],
        out_shape=[jax.ShapeDtypeStruct((B, S, 512), F32),
                   jax.ShapeDtypeStruct((B, KV_GROUPS, LANES, S), BF16),
                   jax.ShapeDtypeStruct((B, nq, 1, LANES), F32)],
        compiler_params=_cparams(("arbitrary", "arbitrary")),
        name="nsa_cmp_select",
    )(_head_rows(SLOPES[HA:], tq), proj, kvc, kvc, proj, ovt)


def _sel_kernel(flag_ref, slope_ref, q_ref, k_ref, v_ref, sel_ref, g_ref, o_ref, m_s, l_s, acc_s, *, tq, nq):
    b = pl.program_id(0)
    i = pl.program_id(1)
    q_stack = _stack_queries(q_ref[0], tq)
    rel = _query_key_offsets(tq, tq)
    slope = slope_ref[...]
    bias0 = slope * rel.astype(F32)
    per_tile = tq // SEL_BLOCK
    key_blk = lax.broadcasted_iota(jnp.int32, (tq, LANES), 0) // SEL_BLOCK
    blk_id = lax.broadcasted_iota(jnp.int32, (tq, LANES), 1)
    m_s[...] = jnp.full(m_s.shape, -jnp.inf, F32)
    l_s[...] = jnp.zeros(l_s.shape, F32)
    acc_s[...] = jnp.zeros(acc_s.shape, F32)

    def tile(kt, diagonal):
        start = pl.multiple_of(kt * tq, tq)
        k_t = k_ref[0, pl.ds(start, tq), :]
        v_t = v_ref[0, pl.ds(start, tq), :]
        expand = jnp.where(blk_id == kt * per_tile + key_blk, 1.0, 0.0).astype(BF16)
        picked = [jnp.dot(expand, sel_ref[0, g], preferred_element_type=F32) for g in range(KV_GROUPS)]
        keep = jnp.concatenate([picked[h // REP] for h in range(N_STACK)], axis=1) > 0.5
        if diagonal:
            keep = keep & (rel >= 0)
        s = _dot_nt(k_t, q_stack) - (bias0 + slope * ((i - kt) * tq).astype(F32))
        s = jnp.where(keep, s, -jnp.inf)
        m, l, acc = _flash_step(s, m_s[...], l_s[...], acc_s[...], v_t)
        m_s[...] = m
        l_s[...] = l
        acc_s[...] = acc

    tile(i, True)

    def body(j, _):
        kt = i - 1 - j

        @pl.when(flag_ref[(b * nq + i) * nq + kt] > 0)
        def _():
            tile(kt, False)

        return 0

    lax.fori_loop(0, i, body, 0)
    gates_t = jax.nn.sigmoid(g_ref[0].astype(F32)).T
    _store_heads(acc_s[...] * (1.0 / l_s[...]), gates_t, 1, o_ref, tq)


def _selected_attention(proj, sel, flags, B, S):
    tq = ATT_TQ
    nq = S // tq
    n = N_STACK * tq
    qb, kb, vb, gbk = SEG_OFF["qb"] // 512, SEG_OFF["kbs"] // 128, SEG_OFF["vbs"] // 128, SEG_OFF["gb"] // 128
    kern = functools.partial(_sel_kernel, tq=tq, nq=nq)
    grid_spec = pltpu.PrefetchScalarGridSpec(
        num_scalar_prefetch=1,
        grid=(B, nq),
        in_specs=[pl.BlockSpec((1, n), lambda b, i, f: (0, 0)),
                  pl.BlockSpec((1, tq, 512), lambda b, i, f: (b, i, qb)),
                  pl.BlockSpec((1, S, 128), lambda b, i, f: (b, 0, kb)),
                  pl.BlockSpec((1, S, 128), lambda b, i, f: (b, 0, vb)),
                  pl.BlockSpec((1, KV_GROUPS, LANES, tq), lambda b, i, f: (b, 0, 0, i)),
                  pl.BlockSpec((1, tq, 128), lambda b, i, f: (b, i, gbk))],
        out_specs=pl.BlockSpec((1, tq, 512), lambda b, i, f: (b, i, 0)),
        scratch_shapes=[pltpu.VMEM((1, n), F32), pltpu.VMEM((1, n), F32), pltpu.VMEM((LANES, n), F32)],
    )
    return pl.pallas_call(
        kern,
        grid_spec=grid_spec,
        out_shape=jax.ShapeDtypeStruct((B, S, 512), F32),
        compiler_params=_cparams(("arbitrary", "arbitrary")),
        name="nsa_selected",
    )(flags, _head_rows(SLOPES[HA:], tq), proj, proj, proj, sel, proj)


def _selection_flags(cnt, B, S):
    nq = S // ATT_TQ
    per_tile = ATT_TQ // SEL_BLOCK
    c = cnt[:, :, 0, :nq * per_tile].reshape(B, nq, nq, per_tile).sum(-1)
    return (c > 0).astype(jnp.int32).reshape(-1)


def _stick_kernel(q_ref, k_ref, v_ref, o_ref, *, t):
    i = pl.program_id(2)
    n = 2 * t
    lane = lax.broadcasted_iota(jnp.int32, (t, LANES), 1)
    q = q_ref[0]
    zero = jnp.zeros_like(q)
    q_stack = jnp.concatenate([jnp.where(lane < HEAD_DIM, q, zero), jnp.where(lane >= HEAD_DIM, q, zero)], axis=0) * SCALE
    key = lax.broadcasted_iota(jnp.int32, (t, n), 0)
    qry = lax.broadcasted_iota(jnp.int32, (t, n), 1) & (t - 1)
    causal = key < qry
    r_i = lax.broadcasted_iota(jnp.int32, (t, t), 0)
    c_i = lax.broadcasted_iota(jnp.int32, (t, t), 1)
    suffix = jnp.where(c_i >= r_i, 1.0, 0.0).astype(BF16)

    def scores(kt, masked):
        start = pl.multiple_of(kt * t, t)
        z = _dot_nt(k_ref[0, pl.ds(start, t), :], q_stack)
        log_1mb = -(jnp.maximum(z, 0.0) + jnp.log(1.0 + jnp.exp(-jnp.abs(z))))
        if masked:
            log_1mb = jnp.where(causal, log_1mb, 0.0)
        hi = log_1mb.astype(BF16)
        lo = (log_1mb - hi.astype(F32)).astype(BF16)
        inc = (jnp.dot(suffix, hi, preferred_element_type=F32)
               + jnp.dot(suffix, lo, preferred_element_type=F32))
        return z, inc

    def weights(z, inc, carry):
        return jnp.exp(z + inc + carry)

    def values(kt, a):
        start = pl.multiple_of(kt * t, t)
        return _dot_tn(v_ref[0, pl.ds(start, t), :], a.astype(BF16))

    prev = jnp.maximum(i - 1, 0)
    z0, inc0 = scores(i, True)
    z1, inc1 = scores(prev, False)
    carry = inc0[0:1, :]
    acc = values(i, jnp.where(causal, weights(z0, inc0, 0.0), 0.0))
    acc = acc + values(prev, jnp.where(i > 0, weights(z1, inc1, carry), 0.0))
    carry = carry + jnp.where(i > 0, inc1[0:1, :], 0.0)

    def cond(state):
        kt, go, _, _ = state
        return (kt >= 0) & (go > 0)

    def body(state):
        kt, _, carry, acc = state
        z, inc = scores(kt, False)
        acc = acc + values(kt, weights(z, inc, carry))
        carry = carry + inc[0:1, :]
        return kt - 1, (jnp.max(carry) > -STICK_EXIT).astype(jnp.int32), carry, acc

    go0 = (jnp.max(carry) > -STICK_EXIT).astype(jnp.int32)
    _, _, _, acc = lax.while_loop(cond, body, (i - 2, go0, carry, acc))
    o_ref[0] = jnp.concatenate([acc[:HEAD_DIM, :t], acc[HEAD_DIM:, t:]], axis=0).T


def _stick_attention(proj, B, S):
    t = min(STICK_T, S)
    qb, kb, vb = SEG_OFF["qc"] // 128, SEG_OFF["kc"] // 128, SEG_OFF["vc"] // 128
    npair = HC // 2
    return pl.pallas_call(
        functools.partial(_stick_kernel, t=t),
        grid=(B, npair, S // t),
        in_specs=[pl.BlockSpec((1, t, 128), lambda b, p, i: (b, i, qb + p)),
                  pl.BlockSpec((1, S, 128), lambda b, p, i: (b, 0, kb + p)),
                  pl.BlockSpec((1, S, 128), lambda b, p, i: (b, 0, vb + p))],
        out_specs=pl.BlockSpec((1, t, 128), lambda b, p, i: (b, i, p)),
        out_shape=jax.ShapeDtypeStruct((B, S, HC * HEAD_DIM), F32),
        compiler_params=_cparams(("arbitrary", "arbitrary", "arbitrary")),
        name="stick_breaking",
    )(proj, proj, proj)


def _group_rms(o, eps=1e-6):
    return o * lax.rsqrt(jnp.mean(o * o, axis=-1, keepdims=True) + eps)


def _outproj_kernel(oa_ref, ob1_ref, ob2_ref, ob3_ref, oc_ref, x_ref, gain_ref, w_ref, g1_ref, lg_ref, lb_ref,
                    sc2_ref, sh2_ref, wr_ref, br_ref, x1_ref, h2_ref, route_ref):
    o_b = ob1_ref[0] + ob2_ref[0] + ob3_ref[0]
    merged = jnp.concatenate([_group_rms(oa_ref[0]), _group_rms(o_b), _group_rms(oc_ref[0])], axis=-1)
    merged = (merged * gain_ref[...]).astype(BF16)
    y = jnp.dot(merged, w_ref[...], preferred_element_type=F32)
    x1 = _layer_norm(ALPHA * x_ref[0] + (1.0 + g1_ref[0]) * y) * lg_ref[...] + lb_ref[...]
    x1_ref[0] = x1
    h2 = _layer_norm(x1) * (1.0 + sc2_ref[0]) + sh2_ref[0]
    h2_ref[0] = h2
    h_hi = h2.astype(BF16)
    h_lo = (h2 - h_hi.astype(F32)).astype(BF16)
    part = jnp.dot(h_hi, wr_ref[...], preferred_element_type=F32)
    logits = (part[:, :LANES] + part[:, LANES:]
              + jnp.dot(h_lo, wr_ref[:, :LANES], preferred_element_type=F32) + br_ref[...])
    tm = logits.shape[0]
    lane = lax.broadcasted_iota(jnp.int32, (tm, LANES), 1)
    lg = jnp.where(lane < N_GROUPS, logits, -jnp.inf)
    mg = jnp.max(lg, axis=-1, keepdims=True)
    p_group = 1.0 / jnp.sum(jnp.exp(lg - mg), axis=-1, keepdims=True)
    g_sel = jnp.min(jnp.where(lg == mg, lane, LANES), axis=-1, keepdims=True)
    e_idx = lane - N_GROUPS
    in_grp = (e_idx >= g_sel * EXPERTS_PER_GROUP) & (e_idx < (g_sel + 1) * EXPERTS_PER_GROUP)
    le = jnp.where(in_grp, logits, -jnp.inf)
    m1 = jnp.max(le, axis=-1, keepdims=True)
    i1 = jnp.min(jnp.where(le == m1, e_idx, LANES), axis=-1, keepdims=True)
    le2 = jnp.where(e_idx == i1, -jnp.inf, le)
    m2 = jnp.max(le2, axis=-1, keepdims=True)
    i2 = jnp.min(jnp.where(le2 == m2, e_idx, LANES), axis=-1, keepdims=True)
    den = jnp.sum(jnp.exp(le - m1), axis=-1, keepdims=True)
    p1 = 1.0 / den
    p2 = jnp.exp(m2 - m1) / den
    gate1 = p_group * p1 / (p1 + p2)
    gate2 = p_group * p2 / (p1 + p2)
    route = jnp.where(lane == 0, i1.astype(F32),
                      jnp.where(lane == 1, i2.astype(F32),
                                jnp.where(lane == 2, gate1, jnp.where(lane == 3, gate2, 0.0))))
    route_ref[0] = route


def _out_projection(o_a, o_b1, o_b2, o_b3, o_c, x, gain, w_out, g1, ln_g, ln_b, sc2, sh2, w_r, b_r, B, S):
    D = D_MODEL
    tm = 256
    row = lambda b, i: (b, i, 0)
    vec = lambda b, i: (0, 0)
    bvec = lambda b, i: (b, 0, 0)
    return pl.pallas_call(
        _outproj_kernel,
        grid=(B, S // tm),
        in_specs=[pl.BlockSpec((1, tm, 512), row), pl.BlockSpec((1, tm, 512), row),
                  pl.BlockSpec((1, tm, 512), row), pl.BlockSpec((1, tm, 512), row),
                  pl.BlockSpec((1, tm, 1024), row), pl.BlockSpec((1, tm, D), row),
                  pl.BlockSpec((1, D), vec), pl.BlockSpec((D, D), vec),
                  pl.BlockSpec((1, 1, D), bvec), pl.BlockSpec((1, D), vec), pl.BlockSpec((1, D), vec),
                  pl.BlockSpec((1, 1, D), bvec), pl.BlockSpec((1, 1, D), bvec),
                  pl.BlockSpec((D, 2 * LANES), vec), pl.BlockSpec((1, LANES), vec)],
        out_specs=[pl.BlockSpec((1, tm, D), row), pl.BlockSpec((1, tm, D), row),
                   pl.BlockSpec((1, tm, LANES), row)],
        out_shape=[jax.ShapeDtypeStruct((B, S, D), F32), jax.ShapeDtypeStruct((B, S, D), F32),
                   jax.ShapeDtypeStruct((B, S, LANES), F32)],
        compiler_params=_cparams(("arbitrary", "arbitrary")),
        name="out_proj_norm_route",
    )(o_a, o_b1, o_b2, o_b3, o_c, x, gain, w_out, g1, ln_g, ln_b, sc2, sh2, w_r, b_r)


ROW_UNROLL = 8


def _row_copy(src_ref, src_row, dst_ref, dst_row, sem):
    return pltpu.make_async_copy(src_ref.at[pl.ds(src_row, 1)], dst_ref.at[pl.ds(dst_row, 1)], sem)


def _dispatch_kernel(dest_ref, pend_ref, h_ref, xs_ref, zero_s, sem, zsem, *, tm, bm):
    i = pl.program_id(0)

    @pl.when(i == 0)
    def _():
        zero_s[...] = jnp.zeros(zero_s.shape, F32)

        def zero_copy(e):
            start = pl.multiple_of(jnp.maximum(pend_ref[e] - bm, 0), bm)
            return pltpu.make_async_copy(zero_s, xs_ref.at[pl.ds(start, bm)], zsem)

        def has_rows(e):
            return pend_ref[e] > (pend_ref[e - 1] if e else 0)

        for e in range(N_EXPERTS):
            pl.when(has_rows(e))(lambda e=e: zero_copy(e).start())
        for e in range(N_EXPERTS):
            pl.when(has_rows(e))(lambda e=e: zero_copy(e).wait())

        def tail_copy(blk):
            return pltpu.make_async_copy(zero_s, xs_ref.at[pl.ds(pl.multiple_of(blk * bm, bm), bm)], zsem)

        def tail_start(blk, carry):
            tail_copy(blk).start()
            return carry

        def tail_wait(blk, carry):
            tail_copy(blk).wait()
            return carry

        first_free = pend_ref[N_EXPERTS - 1] // bm
        lax.fori_loop(first_free, xs_ref.shape[0] // bm, tail_start, 0)
        lax.fori_loop(first_free, xs_ref.shape[0] // bm, tail_wait, 0)

    def body(blk, _):
        for u in range(ROW_UNROLL):
            r = blk * ROW_UNROLL + u
            for k in range(2):
                _row_copy(h_ref, r, xs_ref, dest_ref[(i * tm + r) * 2 + k], sem).start()
        return 0

    lax.fori_loop(0, tm // ROW_UNROLL, body, 0)
    for k in range(2):
        pltpu.make_async_copy(h_ref, xs_ref.at[pl.ds(0, tm)], sem).wait()


def _dispatch(dest, pad_end, h2, rows, bm):
    N, D = h2.shape
    tm = MOE_TM
    grid_spec = pltpu.PrefetchScalarGridSpec(
        num_scalar_prefetch=2,
        grid=(N // tm,),
        in_specs=[pl.BlockSpec((tm, D), lambda i, d, p: (i, 0))],
        out_specs=pl.BlockSpec(memory_space=pl.ANY),
        scratch_shapes=[pltpu.VMEM((bm, D), F32), pltpu.SemaphoreType.DMA(()), pltpu.SemaphoreType.DMA(())],
    )
    return pl.pallas_call(
        functools.partial(_dispatch_kernel, tm=tm, bm=bm),
        grid_spec=grid_spec,
        out_shape=jax.ShapeDtypeStruct((rows, D), F32),
        compiler_params=_cparams(("arbitrary",)),
        name="moe_dispatch",
    )(dest, pad_end, h2)


def _expert_kernel(be_ref, nb_ref, x_ref, wg_ref, wu_ref, wd_ref, y_ref, wg_s, wu_s, wd_s):
    i = pl.program_id(0)

    @pl.when(i < nb_ref[0])
    def _():
        @pl.when((i == 0) | (be_ref[i] != be_ref[jnp.maximum(i - 1, 0)]))
        def _():
            wg_s[...] = wg_ref[0, 0].astype(BF16)
            wu_s[...] = wu_ref[0, 0].astype(BF16)
            wd_s[...] = wd_ref[0, 0].astype(BF16)

        x = x_ref[...].astype(BF16)
        hg = jnp.dot(x, wg_s[...], preferred_element_type=F32)
        hu = jnp.dot(x, wu_s[...], preferred_element_type=F32)
        hid = (hg * jax.nn.sigmoid(hg)) * hu
        y_ref[...] = jnp.dot(hid.astype(BF16), wd_s[...], preferred_element_type=F32)

    @pl.when(i >= nb_ref[0])
    def _():
        y_ref[...] = jnp.zeros(y_ref.shape, F32)


def _expert_mlp(blk_expert, n_used, xs, w_gate, w_up, w_down, layer, bm):
    rows, D = xs.shape
    nblk = rows // bm
    blk = lambda i, be, nb: (jnp.minimum(i, nb[0] - 1), 0)
    wmap = lambda i, be, nb: (layer, be[jnp.minimum(i, nb[0] - 1)], 0, 0)
    grid_spec = pltpu.PrefetchScalarGridSpec(
        num_scalar_prefetch=2,
        grid=(nblk,),
        in_specs=[pl.BlockSpec((bm, D), blk),
                  pl.BlockSpec((1, 1, D, D_EXPERT), wmap),
                  pl.BlockSpec((1, 1, D, D_EXPERT), wmap),
                  pl.BlockSpec((1, 1, D_EXPERT, D), wmap)],
        out_specs=pl.BlockSpec((bm, D), lambda i, be, nb: (i, 0)),
        scratch_shapes=[pltpu.VMEM((D, D_EXPERT), BF16), pltpu.VMEM((D, D_EXPERT), BF16),
                        pltpu.VMEM((D_EXPERT, D), BF16)],
    )
    return pl.pallas_call(
        _expert_kernel,
        grid_spec=grid_spec,
        out_shape=jax.ShapeDtypeStruct((rows, D), F32),
        compiler_params=_cparams(("arbitrary",)),
        name="expert_mlp",
    )(blk_expert, n_used, xs, w_gate, w_up, w_down)


def _combine_kernel(dest_ref, x_ref, route_ref, g2_ref, lg_ref, lb_ref, yb_ref, o_ref, buf, sem, *, tm):
    i = pl.program_id(0)

    def body(blk, _):
        for u in range(ROW_UNROLL):
            r = blk * ROW_UNROLL + u
            for k in range(2):
                _row_copy(yb_ref, dest_ref[(i * tm + r) * 2 + k], buf.at[k], r, sem).start()
        return 0

    lax.fori_loop(0, tm // ROW_UNROLL, body, 0)
    for k in range(2):
        pltpu.make_async_copy(yb_ref.at[pl.ds(0, tm)], buf.at[k], sem).wait()
    route = route_ref[...]
    y = buf[0] * route[:, 2:3] + buf[1] * route[:, 3:4]
    o_ref[...] = _layer_norm(ALPHA * x_ref[...] + (1.0 + g2_ref[0]) * y) * lg_ref[...] + lb_ref[...]


def _combine_post_norm(dest, x1, route, g2, ln_g, ln_b, yb, S):
    N, D = x1.shape
    tm = MOE_TM
    grid_spec = pltpu.PrefetchScalarGridSpec(
        num_scalar_prefetch=1,
        grid=(N // tm,),
        in_specs=[pl.BlockSpec((tm, D), lambda i, d: (i, 0)),
                  pl.BlockSpec((tm, LANES), lambda i, d: (i, 0)),
                  pl.BlockSpec((1, 1, D), lambda i, d: (i * tm // S, 0, 0)),
                  pl.BlockSpec((1, D), lambda i, d: (0, 0)), pl.BlockSpec((1, D), lambda i, d: (0, 0)),
                  pl.BlockSpec(memory_space=pl.ANY)],
        out_specs=pl.BlockSpec((tm, D), lambda i, d: (i, 0)),
        scratch_shapes=[pltpu.VMEM((2, tm, D), F32), pltpu.SemaphoreType.DMA(())],
    )
    return pl.pallas_call(
        functools.partial(_combine_kernel, tm=tm),
        grid_spec=grid_spec,
        out_shape=jax.ShapeDtypeStruct((N, D), F32),
        compiler_params=_cparams(("arbitrary",)),
        name="moe_combine_post_norm",
    )(dest, x1, route, g2, ln_g, ln_b, yb)


MOE_BM = 256
MOE_TM = 256


def _moe(h2, x1, route, g2, ln_g, ln_b, w_gate, w_up, w_down, layer, B, S):
    D = D_MODEL
    N = B * S
    A = 2 * N
    bm = MOE_BM
    r = route.reshape(N, LANES)
    e_flat = r[:, 0:2].astype(jnp.int32).reshape(A)
    onehot = (e_flat[:, None] == jnp.arange(N_EXPERTS)[None, :]).astype(jnp.int32)
    csum = jnp.cumsum(onehot, axis=0)
    counts = csum[-1]
    rank = jnp.sum((csum - onehot) * onehot, axis=1)
    padded = (counts + bm - 1) // bm * bm
    pad_end = jnp.cumsum(padded).astype(jnp.int32)
    pad_start = pad_end - padded
    dest = (pad_start[e_flat] + rank).astype(jnp.int32)
    nblk = -(-A // bm) + N_EXPERTS
    blk_start = jnp.arange(nblk) * bm
    blk_expert = jnp.minimum(jnp.sum(blk_start[:, None] >= pad_end[None, :], axis=1), N_EXPERTS - 1).astype(jnp.int32)
    n_used = (pad_end[-1] // bm).astype(jnp.int32).reshape(1)
    xs = _dispatch(dest, pad_end, h2.reshape(N, D), nblk * bm, bm)
    yb = _expert_mlp(blk_expert, n_used, xs, w_gate, w_up, w_down, layer, bm)
    out = _combine_post_norm(dest, x1.reshape(N, D), r, g2, ln_g, ln_b, yb, S)
    return out.reshape(B, S, D)


def kernel(x, c, w_ada, b_ada, w_in, sinks, cmp_pe_k, cmp_w1_k, cmp_w2_k, cmp_pe_v, cmp_w1_v, cmp_w2_v, mix_gain, w_out, ln1_g, ln1_b, w_rg, b_rg, w_re, b_re, w_gate, w_up, w_down, ln2_g, ln2_b):
    B, S, D = x.shape
    L = w_ada.shape[0]
    mod = _ada_mod(c, w_ada, b_ada)
    for l in range(L):
        sh1, sc1, g1, sh2, sc2, g2 = [mod[l, :, k * D:(k + 1) * D].reshape(B, 1, D) for k in range(6)]
        proj = _in_projection(x.reshape(B * S, D), sc1, sh1, _pack_w_in(w_in[l]), S).reshape(B, S, D_PROJ)
        o_a = _band_attention(proj, "qa", "ka", "va", WINDOW_A, SLOPES[:HA], sinks[l], 0, B, S)
        kvc = _compress(proj, jnp.stack([cmp_pe_k[l], cmp_pe_v[l]]), jnp.stack([cmp_w1_k[l], cmp_w1_v[l]]),
                        jnp.stack([cmp_w2_k[l], cmp_w2_v[l]]), B, S)
        o_cmp, sel, cnt = _cmp_select(proj, kvc, B, S)
        o_slc = _selected_attention(proj, sel, _selection_flags(cnt, B, S), B, S)
        o_win = _band_attention(proj, "qb", "kbw", "vbw", WINDOW_B, SLOPES[HA:], None, 2, B, S)
        o_c = _stick_attention(proj, B, S)
        w_r = jnp.concatenate([w_rg[l], w_re[l].transpose(1, 0, 2).reshape(D, N_EXPERTS),
                               jnp.zeros((D, LANES - N_GROUPS - N_EXPERTS), F32)], axis=1)
        b_r = jnp.concatenate([b_rg[l], b_re[l].reshape(N_EXPERTS),
                               jnp.zeros((LANES - N_GROUPS - N_EXPERTS,), F32)]).reshape(1, LANES)
        w_r_hi = w_r.astype(BF16)
        w_r_lo = (w_r - w_r_hi.astype(F32)).astype(BF16)
        x1, h2, route = _out_projection(o_a, o_cmp, o_slc, o_win, o_c, x, mix_gain[l].reshape(1, D),
                                        w_out[l].astype(BF16), g1, ln1_g[l].reshape(1, D), ln1_b[l].reshape(1, D),
                                        sc2, sh2, jnp.concatenate([w_r_hi, w_r_lo], axis=1), b_r, B, S)
        x = _moe(h2, x1, route, g2, ln2_g[l].reshape(1, D), ln2_b[l].reshape(1, D), w_gate, w_up, w_down, l, B, S)
    return x
```

```python
import functools

import numpy as np
import jax
import jax.numpy as jnp
from jax import lax
from jax.experimental import pallas as pl
from jax.experimental.pallas import tpu as pltpu

F32 = jnp.float32
BF16 = jnp.bfloat16
HIGHEST = lax.Precision.HIGHEST

D_MODEL = 2048
DEPTH = 2
HEAD_DIM = 64
HA = 8
HB = 8
HC = 16
KV_GROUPS = 2
REP = 4
N_STACK = KV_GROUPS * REP
SCALE = HEAD_DIM ** -0.5
WINDOW_A = 128
WINDOW_B = 512
CMP_BLOCK = 32
CMP_STRIDE = 16
CMP_HIDDEN = 128
SEL_BLOCK = 64
N_SELECT = 16
FORCE_SCORE = 1e4
N_GROUPS = 4
EXPERTS_PER_GROUP = 8
N_EXPERTS = 32
D_EXPERT = 512
ALPHA = (2.0 * DEPTH) ** 0.25
LANES = 128
VMEM_LIMIT = 48 * 1024 * 1024

SEG_SIZES = dict(qa=512, qb=512, ka=128, va=128, kbc=128, vbc=128, kbs=128, vbs=128,
                 kbw=128, vbw=128, gb=128, qc=1024, kc=1024, vc=1024)
SEG_ORDER = ("qa", "qb", "ka", "va", "kbc", "vbc", "kbs", "vbs", "kbw", "vbw", "gb", "qc", "kc", "vc")
SEG_OFF = {}
_o = 0
for _n in SEG_ORDER:
    SEG_OFF[_n] = _o
    _o += SEG_SIZES[_n]
PROJ_TN = 768
D_PROJ = -(-_o // PROJ_TN) * PROJ_TN
REF_SEGS = (("qa", 512), ("ka", 128), ("va", 128), ("qb", 512), ("kbc", 128), ("vbc", 128), ("kbs", 128),
            ("vbs", 128), ("kbw", 128), ("vbw", 128), ("gb", 24), ("qc", 1024), ("kc", 1024), ("vc", 1024))

SLOPES = [2.0 ** (-8.0 * (i + 1) / (HA + HB)) for i in range(HA + HB)]
STICK_EXIT = 105.0

ATT_TQ = 256
ATT_TK = 128
CMP_TQ = 128
STICK_T = 256
STICK_PAIRS = 2
M_INIT = -1e30


def _cparams(sem):
    return pltpu.CompilerParams(dimension_semantics=sem, vmem_limit_bytes=VMEM_LIMIT)


def _dot_nt(a, b):
    return lax.dot_general(a, b, (((1,), (1,)), ((), ())), preferred_element_type=F32)


def _dot_tn(a, b):
    return lax.dot_general(a, b, (((0,), (0,)), ((), ())), preferred_element_type=F32)


def _layer_norm(x, eps=1e-5):
    mu = jnp.mean(x, axis=-1, keepdims=True)
    xc = x - mu
    var = jnp.mean(xc * xc, axis=-1, keepdims=True)
    return xc * lax.rsqrt(var + eps)


def _ada_kernel(c_ref, w_ref, b_ref, o_ref):
    c = c_ref[...]
    cs = c * jax.nn.sigmoid(c)
    o_ref[0] = jnp.dot(cs, w_ref[0], preferred_element_type=F32, precision=HIGHEST) + b_ref[0]


def _ada_mod(c, w_ada, b_ada):
    B = c.shape[0]
    L, D, N6 = w_ada.shape
    cp = jnp.zeros((8, D), F32).at[:B].set(c)
    tn = 1024
    out = pl.pallas_call(
        _ada_kernel,
        grid=(L, N6 // tn),
        in_specs=[pl.BlockSpec((8, D), lambda l, j: (0, 0)),
                  pl.BlockSpec((1, D, tn), lambda l, j: (l, 0, j)),
                  pl.BlockSpec((1, 1, tn), lambda l, j: (l, 0, j))],
        out_specs=pl.BlockSpec((1, 8, tn), lambda l, j: (l, 0, j)),
        out_shape=jax.ShapeDtypeStruct((L, 8, N6), F32),
        compiler_params=_cparams(("arbitrary", "arbitrary")),
        name="ada_mod",
    )(cp, w_ada, b_ada.reshape(L, 1, N6))
    return out[:, :B]


def _inproj_kernel(x_ref, sc_ref, sh_ref, w_ref, o_ref, h_ref):
    @pl.when(pl.program_id(1) == 0)
    def _():
        h = _layer_norm(x_ref[...]) * (1.0 + sc_ref[0]) + sh_ref[0]
        h_ref[...] = h.astype(BF16)

    o_ref[...] = jnp.dot(h_ref[...], w_ref[...], preferred_element_type=F32).astype(BF16)


def _in_projection(x2d, sc, sh, w_packed, S):
    N, D = x2d.shape
    tm = 1024 if S % 1024 == 0 else S
    return pl.pallas_call(
        _inproj_kernel,
        grid=(N // tm, D_PROJ // PROJ_TN),
        in_specs=[pl.BlockSpec((tm, D), lambda i, j: (i, 0)),
                  pl.BlockSpec((1, 1, D), lambda i, j: (i * tm // S, 0, 0)),
                  pl.BlockSpec((1, 1, D), lambda i, j: (i * tm // S, 0, 0)),
                  pl.BlockSpec((D, PROJ_TN), lambda i, j: (0, j))],
        out_specs=pl.BlockSpec((tm, PROJ_TN), lambda i, j: (i, j)),
        out_shape=jax.ShapeDtypeStruct((N, D_PROJ), BF16),
        scratch_shapes=[pltpu.VMEM((tm, D), BF16)],
        compiler_params=_cparams(("arbitrary", "arbitrary")),
        name="ln_in_proj",
    )(x2d, sc, sh, w_packed)


def _pack_w_in(w_in_l):
    cols = {}
    off = 0
    for name, width in REF_SEGS:
        cols[name] = w_in_l[:, off:off + width]
        off += width
    parts = []
    for name in SEG_ORDER:
        wseg = cols[name]
        pad = SEG_SIZES[name] - wseg.shape[1]
        if pad:
            wseg = jnp.pad(wseg, ((0, 0), (0, pad)))
        parts.append(wseg)
    total = sum(SEG_SIZES.values())
    parts.append(jnp.zeros((w_in_l.shape[0], D_PROJ - total), w_in_l.dtype))
    return jnp.concatenate(parts, axis=1).astype(BF16)


def _compress_kernel(xa_ref, xb_ref, pe_ref, w1_ref, w2_ref, o_ref):
    half = CMP_STRIDE * HEAD_DIM
    for kv in range(2):
        xa = (xa_ref[kv, 0, 0].astype(F32) + pe_ref[kv, :, :half]).astype(BF16)
        xb = (xb_ref[kv, 0, 0].astype(F32) + pe_ref[kv, :, half:]).astype(BF16)
        hid = (jnp.dot(xa, w1_ref[kv, :half, :], preferred_element_type=F32)
               + jnp.dot(xb, w1_ref[kv, half:, :], preferred_element_type=F32))
        act = jax.nn.gelu(hid)
        o_ref[kv, 0, 0] = jnp.dot(act.astype(BF16), w2_ref[kv], preferred_element_type=F32).astype(BF16)


def _compress(proj, pe_kv, w1_kv, w2_kv, B, S):
    nchunk = S // CMP_STRIDE
    ncp = nchunk

    def chunks(name):
        t = proj[:, :, SEG_OFF[name]:SEG_OFF[name] + 128].reshape(B, nchunk, CMP_STRIDE, KV_GROUPS, HEAD_DIM)
        return t.transpose(0, 3, 1, 2, 4).reshape(B, KV_GROUPS, nchunk, CMP_STRIDE * HEAD_DIM)

    x16 = jnp.stack([chunks("kbc"), chunks("vbc")])
    xa = x16
    xb = jnp.concatenate([x16[:, :, :, 1:], jnp.zeros_like(x16[:, :, :, :1])], axis=3)
    pe = pe_kv.reshape(2, 1, CMP_BLOCK * HEAD_DIM)
    blk = (2, 1, 1, ncp, CMP_STRIDE * HEAD_DIM)
    out = pl.pallas_call(
        _compress_kernel,
        grid=(B, KV_GROUPS),
        in_specs=[pl.BlockSpec(blk, lambda b, g: (0, b, g, 0, 0)),
                  pl.BlockSpec(blk, lambda b, g: (0, b, g, 0, 0)),
                  pl.BlockSpec((2, 1, CMP_BLOCK * HEAD_DIM), lambda b, g: (0, 0, 0)),
                  pl.BlockSpec((2, CMP_BLOCK * HEAD_DIM, CMP_HIDDEN), lambda b, g: (0, 0, 0)),
                  pl.BlockSpec((2, CMP_HIDDEN, HEAD_DIM), lambda b, g: (0, 0, 0))],
        out_specs=pl.BlockSpec((2, 1, 1, ncp, HEAD_DIM), lambda b, g: (0, b, g, 0, 0)),
        out_shape=jax.ShapeDtypeStruct((2, B, KV_GROUPS, ncp, HEAD_DIM), BF16),
        compiler_params=_cparams(("arbitrary", "arbitrary")),
        name="nsa_compress",
    )(xa, xb, pe, w1_kv.astype(BF16), w2_kv.astype(BF16))
    return out.transpose(0, 1, 3, 2, 4).reshape(2, B, ncp, KV_GROUPS * HEAD_DIM)


def _stack_queries(q, tq):
    parts = []
    for h in range(N_STACK):
        q_h = q[:, h * HEAD_DIM:(h + 1) * HEAD_DIM]
        z = jnp.zeros_like(q_h)
        parts.append(jnp.concatenate([q_h, z] if h // REP == 0 else [z, q_h], axis=1))
    return jnp.concatenate(parts, axis=0) * SCALE


def _query_key_offsets(tk, tq):
    key = lax.broadcasted_iota(jnp.int32, (tk, N_STACK * tq), 0)
    qry = lax.broadcasted_iota(jnp.int32, (tk, N_STACK * tq), 1) & (tq - 1)
    return qry - key


def _flash_step(s, m, l, acc, v_t):
    m_new = jnp.maximum(m, jnp.max(s, axis=0, keepdims=True))
    a = jnp.exp(m - m_new)
    p = jnp.exp(s - m_new)
    l_new = a * l + jnp.sum(p, axis=0, keepdims=True)
    return m_new, l_new, a * acc + _dot_tn(v_t, p.astype(BF16))


def _store_heads(acc_n, gates_t, gate_col, o_ref, tq):
    for pair in range(N_STACK // 2):
        g = (2 * pair) // REP
        blocks = []
        for h in (2 * pair, 2 * pair + 1):
            blk = acc_n[g * HEAD_DIM:(g + 1) * HEAD_DIM, h * tq:(h + 1) * tq]
            if gates_t is not None:
                c = h * 3 + gate_col
                blk = blk * gates_t[c:c + 1, :]
            blocks.append(blk)
        o_ref[0, :, pair * LANES:(pair + 1) * LANES] = jnp.concatenate(blocks, axis=0).T


def _head_rows(values, tq):
    return jnp.repeat(jnp.asarray(values, F32), tq).reshape(1, N_STACK * tq)


def _band_kernel(*refs, window, use_sink, gate_col, tq, tk):
    if use_sink:
        slope_ref, sink_ref, q_ref, k_ref, v_ref, o_ref = refs
        g_ref = None
    else:
        slope_ref, q_ref, k_ref, v_ref, g_ref, o_ref = refs
    i = pl.program_id(1)
    n = N_STACK * tq
    per_q = tq // tk
    q_stack = _stack_queries(q_ref[0], tq)
    rel = _query_key_offsets(tk, tq)
    slope = slope_ref[...]
    bias0 = slope * rel.astype(F32)
    if use_sink:
        m, l = sink_ref[...], jnp.ones((1, n), F32)
    else:
        m, l = jnp.full((1, n), M_INIT, F32), jnp.zeros((1, n), F32)
    acc = jnp.zeros((LANES, n), F32)
    last = (i + 1) * per_q - 1
    for j in range(per_q + window // tk):
        c = j * tk - (tq - tk)
        start = pl.multiple_of(jnp.maximum(last - j, 0) * tk, tk)
        k_t = k_ref[0, pl.ds(start, tk), :]
        v_t = v_ref[0, pl.ds(start, tk), :]
        s = _dot_nt(k_t, q_stack) - (bias0 + slope * float(c))
        keep = None
        if c < tk - 1:
            keep = rel >= -c
        if c + tq - 1 >= window:
            far = rel < window - c
            keep = far if keep is None else keep & far
        if keep is not None:
            s = jnp.where(keep, s, -jnp.inf)
        if j >= per_q:
            s = jnp.where(last >= j, s, -jnp.inf)
        m, l, acc = _flash_step(s, m, l, acc, v_t)
    gates_t = None if g_ref is None else jax.nn.sigmoid(g_ref[0].astype(F32)).T
    _store_heads(acc * (1.0 / l), gates_t, gate_col, o_ref, tq)


def _band_attention(proj, qname, kname, vname, window, slopes, sinks, gate_col, B, S):
    tq, tk = ATT_TQ, ATT_TK
    n = N_STACK * tq
    kern = functools.partial(_band_kernel, window=window, use_sink=sinks is not None, gate_col=gate_col, tq=tq, tk=tk)
    qb, kb, vb, gbk = SEG_OFF[qname] // 512, SEG_OFF[kname] // 128, SEG_OFF[vname] // 128, SEG_OFF["gb"] // 128
    row = pl.BlockSpec((1, n), lambda b, i: (0, 0))
    specs = [pl.BlockSpec((1, tq, 512), lambda b, i: (b, i, qb)),
             pl.BlockSpec((1, S, 128), lambda b, i: (b, 0, kb)),
             pl.BlockSpec((1, S, 128), lambda b, i: (b, 0, vb))]
    args = [proj, proj, proj]
    if sinks is not None:
        specs = [row, row] + specs
        args = [_head_rows(slopes, tq), _head_rows(sinks, tq)] + args
    else:
        specs = [row] + specs + [pl.BlockSpec((1, tq, 128), lambda b, i: (b, i, gbk))]
        args = [_head_rows(slopes, tq)] + args + [proj]
    return pl.pallas_call(
        kern,
        grid=(B, S // tq),
        in_specs=specs,
        out_specs=pl.BlockSpec((1, tq, 512), lambda b, i: (b, i, 0)),
        out_shape=jax.ShapeDtypeStruct((B, S, 512), F32),
        compiler_params=_cparams(("arbitrary", "arbitrary")),
        name="band_attn_w%d" % window,
    )(*args)


def _cmp_select_kernel(slope_ref, q_ref, kc_ref, vc_ref, g_ref, ovt_ref, o_ref, sel_ref, cnt_ref, *, tq, n_pick):
    i = pl.program_id(1)
    ncp = kc_ref.shape[2]
    n = N_STACK * tq
    q_stack = _stack_queries(q_ref[0], tq)
    t_row = i * tq + (lax.broadcasted_iota(jnp.int32, (1, n), 1) & (tq - 1))
    n_idx = lax.broadcasted_iota(jnp.int32, (ncp, n), 0)
    last_visible = (t_row - (CMP_BLOCK - 1)) >> (CMP_STRIDE.bit_length() - 1)
    s = _dot_nt(kc_ref[0, 0], q_stack) + slope_ref[...]
    s = jnp.where(n_idx <= last_visible, s, -jnp.inf)
    m = jnp.max(s, axis=0, keepdims=True)
    m = jnp.where(m == -jnp.inf, 0.0, m)
    e = jnp.exp(s - m)
    den = jnp.sum(e, axis=0, keepdims=True)
    p = e * (1.0 / jnp.maximum(den, 1e-30))
    gates_t = jax.nn.sigmoid(g_ref[0].astype(F32)).T
    _store_heads(_dot_tn(vc_ref[0, 0], p.astype(BF16)), gates_t, 0, o_ref, tq)
    psum = jnp.concatenate(
        [sum(p[:, (g * REP + r) * tq:(g * REP + r + 1) * tq] for r in range(REP)) for g in range(KV_GROUPS)], axis=1)
    p_hi = psum.astype(BF16)
    p_lo = (psum - p_hi.astype(F32)).astype(BF16)
    imp = (jnp.dot(ovt_ref[...], p_hi, preferred_element_type=F32)
           + jnp.dot(ovt_ref[...], p_lo, preferred_element_type=F32))
    w = KV_GROUPS * tq
    j_idx = lax.broadcasted_iota(jnp.int32, (LANES, w), 0)
    t_sel = i * tq + (lax.broadcasted_iota(jnp.int32, (1, w), 1) & (tq - 1))
    cur = t_sel // SEL_BLOCK
    forced = (j_idx == 0) | (j_idx == cur) | (j_idx == cur - 1)
    valid = j_idx * SEL_BLOCK <= t_sel
    imp = jnp.where(valid, jnp.where(forced, FORCE_SCORE, imp), -1.0)

    def pick(_, carry):
        imp, sel = carry
        mx = jnp.max(imp, axis=0, keepdims=True)
        first = jnp.min(jnp.where(imp == mx, j_idx, LANES), axis=0, keepdims=True)
        hit = j_idx == first
        return jnp.where(hit, -2.0, imp), jnp.where(hit, 1.0, sel)

    _, sel = lax.fori_loop(0, n_pick, pick, (imp, jnp.zeros((LANES, w), F32)))
    sel_b = sel.astype(BF16)
    for g in range(KV_GROUPS):
        sel_ref[0, g] = sel_b[:, g * tq:(g + 1) * tq]
    cnt = _dot_nt(jnp.ones((8, w), BF16), sel_b)
    cnt_ref[0, 0] = cnt[0:1, :]


def _overlap_matrix_t(ncp):
    n = np.arange(ncp)[None, :]
    j = np.arange(LANES)[:, None]
    start = n * CMP_STRIDE
    end = start + CMP_BLOCK - 1
    return ((end >= j * SEL_BLOCK) & (start < j * SEL_BLOCK + SEL_BLOCK)).astype(np.float32)


def _cmp_select(proj, kvc, B, S):
    tq = CMP_TQ
    nq = S // tq
    ncp = kvc.shape[2]
    n = N_STACK * tq
    n_pick = min(N_SELECT, S // SEL_BLOCK)
    qb, gbk = SEG_OFF["qb"] // 512, SEG_OFF["gb"] // 128
    ovt = jnp.asarray(_overlap_matrix_t(ncp), BF16)
    block_end = jnp.arange(ncp, dtype=F32).reshape(ncp, 1) * CMP_STRIDE + (CMP_BLOCK - 1)
    key_bias = block_end * _head_rows(SLOPES[HA:], tq)
    kern = functools.partial(_cmp_select_kernel, tq=tq, n_pick=n_pick)
    kv_blk = (1, 1, ncp, LANES)
    return pl.pallas_call(
        kern,
        grid=(B, nq),
        in_specs=[pl.BlockSpec((ncp, n), lambda b, i: (0, 0)),
                  pl.BlockSpec((1, tq, 512), lambda b, i: (b, i, qb)),
                  pl.BlockSpec(kv_blk, lambda b, i: (0, b, 0, 0)),
                  pl.BlockSpec(kv_blk, lambda b, i: (1, b, 0, 0)),
                  pl.BlockSpec((1, tq, 128), lambda b, i: (b, i, gbk)),
                  pl.BlockSpec((LANES, ncp), lambda b, i: (0, 0))],
        out_specs=[pl.BlockSpec((1, tq, 512), lambda b, i: (b, i, 0)),
                   pl.BlockSpec((1, KV_GROUPS, LANES, tq), lambda b, i: (b, 0, 0, i)),
                   pl.BlockSpec((1, 1, 1, LANES), lambda b, i: (b, i, 0, 0))],
        out_shape=[jax.ShapeDtypeStruct((B, S, 512), F32),
                   jax.ShapeDtypeStruct((B, KV_GROUPS, LANES, S), BF16),
                   jax.ShapeDtypeStruct((B, nq, 1, LANES), F32)],
        compiler_params=_cparams(("arbitrary", "arbitrary")),
        name="nsa_cmp_select",
    )(key_bias, proj, kvc, kvc, proj, ovt)


def _sel_kernel(flag_ref, slope_ref, q_ref, k_ref, v_ref, sel_ref, g_ref, o_ref, m_s, l_s, acc_s, *, tq, tk, nq):
    b = pl.program_id(0)
    i = pl.program_id(1)
    nk = nq * (tq // tk)
    per_q = tq // tk
    q_stack = _stack_queries(q_ref[0], tq)
    rel = _query_key_offsets(tk, tq)
    slope = slope_ref[...]
    bias0 = slope * rel.astype(F32)
    per_tile = tk // SEL_BLOCK
    key_blk = lax.broadcasted_iota(jnp.int32, (tk, LANES), 0) // SEL_BLOCK
    blk_id = lax.broadcasted_iota(jnp.int32, (tk, LANES), 1)
    m_s[...] = jnp.full(m_s.shape, M_INIT, F32)
    l_s[...] = jnp.zeros(l_s.shape, F32)
    acc_s[...] = jnp.zeros(acc_s.shape, F32)

    def tile(kt, diagonal):
        start = pl.multiple_of(kt * tk, tk)
        k_t = k_ref[0, pl.ds(start, tk), :]
        v_t = v_ref[0, pl.ds(start, tk), :]
        expand = jnp.where(blk_id == kt * per_tile + key_blk, 1.0, 0.0).astype(BF16)
        picked = [jnp.dot(expand, sel_ref[0, g], preferred_element_type=F32) for g in range(KV_GROUPS)]
        keep = jnp.concatenate([picked[h // REP] for h in range(N_STACK)], axis=1) > 0.5
        shift = i * tq - kt * tk
        if diagonal:
            keep = keep & (rel >= -shift)
        s = _dot_nt(k_t, q_stack) - (bias0 + slope * shift.astype(F32))
        s = jnp.where(keep, s, -jnp.inf)
        m, l, acc = _flash_step(s, m_s[...], l_s[...], acc_s[...], v_t)
        m_s[...] = m
        l_s[...] = l
        acc_s[...] = acc

    for d in range(per_q):
        tile((i + 1) * per_q - 1 - d, True)

    def body(j, _):
        kt = i * per_q - 1 - j

        @pl.when(flag_ref[(b * nq + i) * nk + kt] > 0)
        def _():
            tile(kt, False)

        return 0

    lax.fori_loop(0, i * per_q, body, 0)
    gates_t = jax.nn.sigmoid(g_ref[0].astype(F32)).T
    _store_heads(acc_s[...] * (1.0 / l_s[...]), gates_t, 1, o_ref, tq)


def _selected_attention(proj, sel, flags, B, S):
    tq, tk = ATT_TQ, ATT_TK
    nq = S // tq
    n = N_STACK * tq
    qb, kb, vb, gbk = SEG_OFF["qb"] // 512, SEG_OFF["kbs"] // 128, SEG_OFF["vbs"] // 128, SEG_OFF["gb"] // 128
    kern = functools.partial(_sel_kernel, tq=tq, tk=tk, nq=nq)
    grid_spec = pltpu.PrefetchScalarGridSpec(
        num_scalar_prefetch=1,
        grid=(B, nq),
        in_specs=[pl.BlockSpec((1, n), lambda b, i, f: (0, 0)),
                  pl.BlockSpec((1, tq, 512), lambda b, i, f: (b, i, qb)),
                  pl.BlockSpec((1, S, 128), lambda b, i, f: (b, 0, kb)),
                  pl.BlockSpec((1, S, 128), lambda b, i, f: (b, 0, vb)),
                  pl.BlockSpec((1, KV_GROUPS, LANES, tq), lambda b, i, f: (b, 0, 0, i)),
                  pl.BlockSpec((1, tq, 128), lambda b, i, f: (b, i, gbk))],
        out_specs=pl.BlockSpec((1, tq, 512), lambda b, i, f: (b, i, 0)),
        scratch_shapes=[pltpu.VMEM((1, n), F32), pltpu.VMEM((1, n), F32), pltpu.VMEM((LANES, n), F32)],
    )
    return pl.pallas_call(
        kern,
        grid_spec=grid_spec,
        out_shape=jax.ShapeDtypeStruct((B, S, 512), F32),
        compiler_params=_cparams(("arbitrary", "arbitrary")),
        name="nsa_selected",
    )(flags, _head_rows(SLOPES[HA:], tq), proj, proj, proj, sel, proj)


def _selection_flags(cnt, B, S):
    nq, nk = S // ATT_TQ, S // ATT_TK
    per_tile = ATT_TK // SEL_BLOCK
    c = cnt[:, :, 0, :nk * per_tile].reshape(B, nq, ATT_TQ // CMP_TQ, nk, per_tile).sum((2, 4))
    return (c > 0).astype(jnp.int32).reshape(-1)


def _stick_kernel(*refs, t, pairs):
    o_ref = refs[3 * pairs]
    i = pl.program_id(2)
    started = [_stick_first_tiles(i, refs[p], refs[pairs + p], refs[2 * pairs + p], t) for p in range(pairs)]
    for p, (carry, acc, step) in enumerate(started):
        def cond(state):
            kt, go, _, _ = state
            return (kt >= 0) & (go > 0)

        def body(state, step=step):
            kt, _, carry, acc = state
            carry, acc = step(kt, carry, acc)
            return kt - 1, (jnp.max(carry) > -STICK_EXIT).astype(jnp.int32), carry, acc

        go0 = (jnp.max(carry) > -STICK_EXIT).astype(jnp.int32)
        _, _, _, acc = lax.while_loop(cond, body, (i - 2, go0, carry, acc))
        o_ref[0, :, p * LANES:(p + 1) * LANES] = jnp.concatenate([acc[:HEAD_DIM, :t], acc[HEAD_DIM:, t:]], axis=0).T


def _stick_first_tiles(i, q_ref, k_ref, v_ref, t):
    n = 2 * t
    lane = lax.broadcasted_iota(jnp.int32, (t, LANES), 1)
    q = q_ref[0]
    zero = jnp.zeros_like(q)
    q_stack = jnp.concatenate([jnp.where(lane < HEAD_DIM, q, zero), jnp.where(lane >= HEAD_DIM, q, zero)], axis=0) * SCALE
    key = lax.broadcasted_iota(jnp.int32, (t, n), 0)
    qry = lax.broadcasted_iota(jnp.int32, (t, n), 1) & (t - 1)
    causal = key < qry
    r_i = lax.broadcasted_iota(jnp.int32, (t, t), 0)
    c_i = lax.broadcasted_iota(jnp.int32, (t, t), 1)
    suffix = jnp.where(c_i >= r_i, 1.0, 0.0).astype(BF16)

    def scores(kt, masked):
        start = pl.multiple_of(kt * t, t)
        z = _dot_nt(k_ref[0, pl.ds(start, t), :], q_stack)
        log_1mb = -(jnp.maximum(z, 0.0) + jnp.log(1.0 + jnp.exp(-jnp.abs(z))))
        if masked:
            log_1mb = jnp.where(causal, log_1mb, 0.0)
        hi = log_1mb.astype(BF16)
        lo = (log_1mb - hi.astype(F32)).astype(BF16)
        inc = (jnp.dot(suffix, hi, preferred_element_type=F32)
               + jnp.dot(suffix, lo, preferred_element_type=F32))
        return z, inc

    def weights(z, inc, carry):
        return jnp.exp(z + inc + carry)

    def values(kt, a):
        start = pl.multiple_of(kt * t, t)
        return _dot_tn(v_ref[0, pl.ds(start, t), :], a.astype(BF16))

    prev = jnp.maximum(i - 1, 0)
    z0, inc0 = scores(i, True)
    z1, inc1 = scores(prev, False)
    carry = inc0[0:1, :]
    acc = values(i, jnp.where(causal, weights(z0, inc0, 0.0), 0.0))
    acc = acc + values(prev, jnp.where(i > 0, weights(z1, inc1, carry), 0.0))
    carry = carry + jnp.where(i > 0, inc1[0:1, :], 0.0)

    def step(kt, carry, acc):
        z, inc = scores(kt, False)
        return carry + inc[0:1, :], acc + values(kt, weights(z, inc, carry))

    return carry, acc, step


def _stick_attention(proj, B, S):
    t = min(STICK_T, S)
    pairs = STICK_PAIRS
    qb, kb, vb = SEG_OFF["qc"] // 128, SEG_OFF["kc"] // 128, SEG_OFF["vc"] // 128
    col = lambda base, p: (lambda b, g, i: (b, i, base + g * pairs + p))
    whole = lambda base, p: (lambda b, g, i: (b, 0, base + g * pairs + p))
    specs = ([pl.BlockSpec((1, t, 128), col(qb, p)) for p in range(pairs)]
             + [pl.BlockSpec((1, S, 128), whole(kb, p)) for p in range(pairs)]
             + [pl.BlockSpec((1, S, 128), whole(vb, p)) for p in range(pairs)])
    return pl.pallas_call(
        functools.partial(_stick_kernel, t=t, pairs=pairs),
        grid=(B, HC // 2 // pairs, S // t),
        in_specs=specs,
        out_specs=pl.BlockSpec((1, t, pairs * LANES), lambda b, g, i: (b, i, g)),
        out_shape=jax.ShapeDtypeStruct((B, S, HC * HEAD_DIM), F32),
        compiler_params=_cparams(("arbitrary", "arbitrary", "arbitrary")),
        name="stick_breaking",
    )(*([proj] * (3 * pairs)))


def _group_rms(o, eps=1e-6):
    return o * lax.rsqrt(jnp.mean(o * o, axis=-1, keepdims=True) + eps)


def _outproj_kernel(oa_ref, ob1_ref, ob2_ref, ob3_ref, oc_ref, x_ref, gain_ref, w_ref, g1_ref, lg_ref, lb_ref,
                    sc2_ref, sh2_ref, wr_ref, br_ref, x1_ref, h2_ref, route_ref):
    o_b = ob1_ref[0] + ob2_ref[0] + ob3_ref[0]
    merged = jnp.concatenate([_group_rms(oa_ref[0]), _group_rms(o_b), _group_rms(oc_ref[0])], axis=-1)
    merged = (merged * gain_ref[...]).astype(BF16)
    y = jnp.dot(merged, w_ref[...], preferred_element_type=F32)
    x1 = _layer_norm(ALPHA * x_ref[0] + (1.0 + g1_ref[0]) * y) * lg_ref[...] + lb_ref[...]
    x1_ref[0] = x1
    h2 = _layer_norm(x1) * (1.0 + sc2_ref[0]) + sh2_ref[0]
    h2_ref[0] = h2
    h_hi = h2.astype(BF16)
    h_lo = (h2 - h_hi.astype(F32)).astype(BF16)
    part = jnp.dot(h_hi, wr_ref[...], preferred_element_type=F32)
    logits = (part[:, :LANES] + part[:, LANES:]
              + jnp.dot(h_lo, wr_ref[:, :LANES], preferred_element_type=F32) + br_ref[...])
    tm = logits.shape[0]
    lane = lax.broadcasted_iota(jnp.int32, (tm, LANES), 1)
    lg = jnp.where(lane < N_GROUPS, logits, -jnp.inf)
    mg = jnp.max(lg, axis=-1, keepdims=True)
    p_group = 1.0 / jnp.sum(jnp.exp(lg - mg), axis=-1, keepdims=True)
    g_sel = jnp.min(jnp.where(lg == mg, lane, LANES), axis=-1, keepdims=True)
    e_idx = lane - N_GROUPS
    in_grp = (e_idx >= g_sel * EXPERTS_PER_GROUP) & (e_idx < (g_sel + 1) * EXPERTS_PER_GROUP)
    le = jnp.where(in_grp, logits, -jnp.inf)
    m1 = jnp.max(le, axis=-1, keepdims=True)
    i1 = jnp.min(jnp.where(le == m1, e_idx, LANES), axis=-1, keepdims=True)
    le2 = jnp.where(e_idx == i1, -jnp.inf, le)
    m2 = jnp.max(le2, axis=-1, keepdims=True)
    i2 = jnp.min(jnp.where(le2 == m2, e_idx, LANES), axis=-1, keepdims=True)
    den = jnp.sum(jnp.exp(le - m1), axis=-1, keepdims=True)
    p1 = 1.0 / den
    p2 = jnp.exp(m2 - m1) / den
    gate1 = p_group * p1 / (p1 + p2)
    gate2 = p_group * p2 / (p1 + p2)
    route = jnp.where(lane == 0, i1.astype(F32),
                      jnp.where(lane == 1, i2.astype(F32),
                                jnp.where(lane == 2, gate1, jnp.where(lane == 3, gate2, 0.0))))
    route_ref[0] = route


def _out_projection(o_a, o_b1, o_b2, o_b3, o_c, x, gain, w_out, g1, ln_g, ln_b, sc2, sh2, w_r, b_r, B, S):
    D = D_MODEL
    tm = 256
    row = lambda b, i: (b, i, 0)
    vec = lambda b, i: (0, 0)
    bvec = lambda b, i: (b, 0, 0)
    return pl.pallas_call(
        _outproj_kernel,
        grid=(B, S // tm),
        in_specs=[pl.BlockSpec((1, tm, 512), row), pl.BlockSpec((1, tm, 512), row),
                  pl.BlockSpec((1, tm, 512), row), pl.BlockSpec((1, tm, 512), row),
                  pl.BlockSpec((1, tm, 1024), row), pl.BlockSpec((1, tm, D), row),
                  pl.BlockSpec((1, D), vec), pl.BlockSpec((D, D), vec),
                  pl.BlockSpec((1, 1, D), bvec), pl.BlockSpec((1, D), vec), pl.BlockSpec((1, D), vec),
                  pl.BlockSpec((1, 1, D), bvec), pl.BlockSpec((1, 1, D), bvec),
                  pl.BlockSpec((D, 2 * LANES), vec), pl.BlockSpec((1, LANES), vec)],
        out_specs=[pl.BlockSpec((1, tm, D), row), pl.BlockSpec((1, tm, D), row),
                   pl.BlockSpec((1, tm, LANES), row)],
        out_shape=[jax.ShapeDtypeStruct((B, S, D), F32), jax.ShapeDtypeStruct((B, S, D), F32),
                   jax.ShapeDtypeStruct((B, S, LANES), F32)],
        compiler_params=_cparams(("arbitrary", "arbitrary")),
        name="out_proj_norm_route",
    )(o_a, o_b1, o_b2, o_b3, o_c, x, gain, w_out, g1, ln_g, ln_b, sc2, sh2, w_r, b_r)


ROW_UNROLL = 8


def _row_copy(src_ref, src_row, dst_ref, dst_row, sem):
    return pltpu.make_async_copy(src_ref.at[pl.ds(src_row, 1)], dst_ref.at[pl.ds(dst_row, 1)], sem)


def _dispatch_kernel(dest_ref, pend_ref, h_ref, xs_ref, zero_s, sem, zsem, *, tm, bm):
    i = pl.program_id(0)

    @pl.when(i == 0)
    def _():
        zero_s[...] = jnp.zeros(zero_s.shape, F32)

        def zero_copy(e):
            start = pl.multiple_of(jnp.maximum(pend_ref[e] - bm, 0), bm)
            return pltpu.make_async_copy(zero_s, xs_ref.at[pl.ds(start, bm)], zsem)

        def has_rows(e):
            return pend_ref[e] > (pend_ref[e - 1] if e else 0)

        for e in range(N_EXPERTS):
            pl.when(has_rows(e))(lambda e=e: zero_copy(e).start())
        for e in range(N_EXPERTS):
            pl.when(has_rows(e))(lambda e=e: zero_copy(e).wait())

        def tail_copy(blk):
            return pltpu.make_async_copy(zero_s, xs_ref.at[pl.ds(pl.multiple_of(blk * bm, bm), bm)], zsem)

        def tail_start(blk, carry):
            tail_copy(blk).start()
            return carry

        def tail_wait(blk, carry):
            tail_copy(blk).wait()
            return carry

        first_free = pend_ref[N_EXPERTS - 1] // bm
        lax.fori_loop(first_free, xs_ref.shape[0] // bm, tail_start, 0)
        lax.fori_loop(first_free, xs_ref.shape[0] // bm, tail_wait, 0)

    def body(blk, _):
        for u in range(ROW_UNROLL):
            r = blk * ROW_UNROLL + u
            for k in range(2):
                _row_copy(h_ref, r, xs_ref, dest_ref[(i * tm + r) * 2 + k], sem).start()
        return 0

    lax.fori_loop(0, tm // ROW_UNROLL, body, 0)
    for k in range(2):
        pltpu.make_async_copy(h_ref, xs_ref.at[pl.ds(0, tm)], sem).wait()


def _dispatch(dest, pad_end, h2, rows, bm):
    N, D = h2.shape
    tm = MOE_TM
    grid_spec = pltpu.PrefetchScalarGridSpec(
        num_scalar_prefetch=2,
        grid=(N // tm,),
        in_specs=[pl.BlockSpec((tm, D), lambda i, d, p: (i, 0))],
        out_specs=pl.BlockSpec(memory_space=pl.ANY),
        scratch_shapes=[pltpu.VMEM((bm, D), F32), pltpu.SemaphoreType.DMA(()), pltpu.SemaphoreType.DMA(())],
    )
    return pl.pallas_call(
        functools.partial(_dispatch_kernel, tm=tm, bm=bm),
        grid_spec=grid_spec,
        out_shape=jax.ShapeDtypeStruct((rows, D), F32),
        compiler_params=_cparams(("arbitrary",)),
        name="moe_dispatch",
    )(dest, pad_end, h2)


def _expert_kernel(be_ref, nb_ref, x_ref, wg_ref, wu_ref, wd_ref, y_ref, wg_s, wu_s, wd_s):
    i = pl.program_id(0)

    @pl.when(i < nb_ref[0])
    def _():
        @pl.when((i == 0) | (be_ref[i] != be_ref[jnp.maximum(i - 1, 0)]))
        def _():
            wg_s[...] = wg_ref[0, 0].astype(BF16)
            wu_s[...] = wu_ref[0, 0].astype(BF16)
            wd_s[...] = wd_ref[0, 0].astype(BF16)

        x = x_ref[...].astype(BF16)
        hg = jnp.dot(x, wg_s[...], preferred_element_type=F32)
        hu = jnp.dot(x, wu_s[...], preferred_element_type=F32)
        hid = (hg * jax.nn.sigmoid(hg)) * hu
        y_ref[...] = jnp.dot(hid.astype(BF16), wd_s[...], preferred_element_type=F32)

    @pl.when(i >= nb_ref[0])
    def _():
        y_ref[...] = jnp.zeros(y_ref.shape, F32)


def _expert_mlp(blk_expert, n_used, xs, w_gate, w_up, w_down, layer, bm):
    rows, D = xs.shape
    nblk = rows // bm
    blk = lambda i, be, nb: (jnp.minimum(i, nb[0] - 1), 0)
    wmap = lambda i, be, nb: (layer, be[jnp.minimum(i, nb[0] - 1)], 0, 0)
    grid_spec = pltpu.PrefetchScalarGridSpec(
        num_scalar_prefetch=2,
        grid=(nblk,),
        in_specs=[pl.BlockSpec((bm, D), blk),
                  pl.BlockSpec((1, 1, D, D_EXPERT), wmap),
                  pl.BlockSpec((1, 1, D, D_EXPERT), wmap),
                  pl.BlockSpec((1, 1, D_EXPERT, D), wmap)],
        out_specs=pl.BlockSpec((bm, D), lambda i, be, nb: (i, 0)),
        scratch_shapes=[pltpu.VMEM((D, D_EXPERT), BF16), pltpu.VMEM((D, D_EXPERT), BF16),
                        pltpu.VMEM((D_EXPERT, D), BF16)],
    )
    return pl.pallas_call(
        _expert_kernel,
        grid_spec=grid_spec,
        out_shape=jax.ShapeDtypeStruct((rows, D), F32),
        compiler_params=_cparams(("arbitrary",)),
        name="expert_mlp",
    )(blk_expert, n_used, xs, w_gate, w_up, w_down)


def _combine_kernel(dest_ref, x_ref, route_ref, g2_ref, lg_ref, lb_ref, yb_ref, o_ref, buf, sem, *, tm):
    i = pl.program_id(0)

    def body(blk, _):
        for u in range(ROW_UNROLL):
            r = blk * ROW_UNROLL + u
            for k in range(2):
                _row_copy(yb_ref, dest_ref[(i * tm + r) * 2 + k], buf.at[k], r, sem).start()
        return 0

    lax.fori_loop(0, tm // ROW_UNROLL, body, 0)
    for k in range(2):
        pltpu.make_async_copy(yb_ref.at[pl.ds(0, tm)], buf.at[k], sem).wait()
    route = route_ref[...]
    y = buf[0] * route[:, 2:3] + buf[1] * route[:, 3:4]
    o_ref[...] = _layer_norm(ALPHA * x_ref[...] + (1.0 + g2_ref[0]) * y) * lg_ref[...] + lb_ref[...]


def _combine_post_norm(dest, x1, route, g2, ln_g, ln_b, yb, S):
    N, D = x1.shape
    tm = MOE_TM
    grid_spec = pltpu.PrefetchScalarGridSpec(
        num_scalar_prefetch=1,
        grid=(N // tm,),
        in_specs=[pl.BlockSpec((tm, D), lambda i, d: (i, 0)),
                  pl.BlockSpec((tm, LANES), lambda i, d: (i, 0)),
                  pl.BlockSpec((1, 1, D), lambda i, d: (i * tm // S, 0, 0)),
                  pl.BlockSpec((1, D), lambda i, d: (0, 0)), pl.BlockSpec((1, D), lambda i, d: (0, 0)),
                  pl.BlockSpec(memory_space=pl.ANY)],
        out_specs=pl.BlockSpec((tm, D), lambda i, d: (i, 0)),
        scratch_shapes=[pltpu.VMEM((2, tm, D), F32), pltpu.SemaphoreType.DMA(())],
    )
    return pl.pallas_call(
        functools.partial(_combine_kernel, tm=tm),
        grid_spec=grid_spec,
        out_shape=jax.ShapeDtypeStruct((N, D), F32),
        compiler_params=_cparams(("arbitrary",)),
        name="moe_combine_post_norm",
    )(dest, x1, route, g2, ln_g, ln_b, yb)


MOE_BM = 256
MOE_TM = 256


def _moe(h2, x1, route, g2, ln_g, ln_b, w_gate, w_up, w_down, layer, B, S):
    D = D_MODEL
    N = B * S
    A = 2 * N
    bm = MOE_BM
    r = route.reshape(N, LANES)
    e_flat = r[:, 0:2].astype(jnp.int32).reshape(A)
    onehot = (e_flat[:, None] == jnp.arange(N_EXPERTS)[None, :]).astype(jnp.int32)
    csum = jnp.cumsum(onehot, axis=0)
    counts = csum[-1]
    rank = jnp.sum((csum - onehot) * onehot, axis=1)
    padded = (counts + bm - 1) // bm * bm
    pad_end = jnp.cumsum(padded).astype(jnp.int32)
    pad_start = pad_end - padded
    dest = (pad_start[e_flat] + rank).astype(jnp.int32)
    nblk = -(-A // bm) + N_EXPERTS
    blk_start = jnp.arange(nblk) * bm
    blk_expert = jnp.minimum(jnp.sum(blk_start[:, None] >= pad_end[None, :], axis=1), N_EXPERTS - 1).astype(jnp.int32)
    n_used = (pad_end[-1] // bm).astype(jnp.int32).reshape(1)
    xs = _dispatch(dest, pad_end, h2.reshape(N, D), nblk * bm, bm)
    yb = _expert_mlp(blk_expert, n_used, xs, w_gate, w_up, w_down, layer, bm)
    out = _combine_post_norm(dest, x1.reshape(N, D), r, g2, ln_g, ln_b, yb, S)
    return out.reshape(B, S, D)


def kernel(x, c, w_ada, b_ada, w_in, sinks, cmp_pe_k, cmp_w1_k, cmp_w2_k, cmp_pe_v, cmp_w1_v, cmp_w2_v, mix_gain, w_out, ln1_g, ln1_b, w_rg, b_rg, w_re, b_re, w_gate, w_up, w_down, ln2_g, ln2_b):
    B, S, D = x.shape
    L = w_ada.shape[0]
    mod = _ada_mod(c, w_ada, b_ada)
    for l in range(L):
        sh1, sc1, g1, sh2, sc2, g2 = [mod[l, :, k * D:(k + 1) * D].reshape(B, 1, D) for k in range(6)]
        proj = _in_projection(x.reshape(B * S, D), sc1, sh1, _pack_w_in(w_in[l]), S).reshape(B, S, D_PROJ)
        o_a = _band_attention(proj, "qa", "ka", "va", WINDOW_A, SLOPES[:HA], sinks[l], 0, B, S)
        kvc = _compress(proj, jnp.stack([cmp_pe_k[l], cmp_pe_v[l]]), jnp.stack([cmp_w1_k[l], cmp_w1_v[l]]),
                        jnp.stack([cmp_w2_k[l], cmp_w2_v[l]]), B, S)
        o_cmp, sel, cnt = _cmp_select(proj, kvc, B, S)
        o_slc = _selected_attention(proj, sel, _selection_flags(cnt, B, S), B, S)
        o_win = _band_attention(proj, "qb", "kbw", "vbw", WINDOW_B, SLOPES[HA:], None, 2, B, S)
        o_c = _stick_attention(proj, B, S)
        w_r = jnp.concatenate([w_rg[l], w_re[l].transpose(1, 0, 2).reshape(D, N_EXPERTS),
                               jnp.zeros((D, LANES - N_GROUPS - N_EXPERTS), F32)], axis=1)
        b_r = jnp.concatenate([b_rg[l], b_re[l].reshape(N_EXPERTS),
                               jnp.zeros((LANES - N_GROUPS - N_EXPERTS,), F32)]).reshape(1, LANES)
        w_r_hi = w_r.astype(BF16)
        w_r_lo = (w_r - w_r_hi.astype(F32)).astype(BF16)
        x1, h2, route = _out_projection(o_a, o_cmp, o_slc, o_win, o_c, x, mix_gain[l].reshape(1, D),
                                        w_out[l].astype(BF16), g1, ln1_g[l].reshape(1, D), ln1_b[l].reshape(1, D),
                                        sc2, sh2, jnp.concatenate([w_r_hi, w_r_lo], axis=1), b_r, B, S)
        x = _moe(h2, x1, route, g2, ln2_g[l].reshape(1, D), ln2_b[l].reshape(1, D), w_gate, w_up, w_down, l, B, S)
    return x
```

```python
import functools

import numpy as np
import jax
import jax.numpy as jnp
from jax import lax
from jax.experimental import pallas as pl
from jax.experimental.pallas import tpu as pltpu

F32 = jnp.float32
BF16 = jnp.bfloat16
HIGHEST = lax.Precision.HIGHEST

D_MODEL = 2048
DEPTH = 2
HEAD_DIM = 64
HA = 8
HB = 8
HC = 16
KV_GROUPS = 2
REP = 4
N_STACK = KV_GROUPS * REP
SCALE = HEAD_DIM ** -0.5
WINDOW_A = 128
WINDOW_B = 512
CMP_BLOCK = 32
CMP_STRIDE = 16
CMP_HIDDEN = 128
SEL_BLOCK = 64
N_SELECT = 16
FORCE_SCORE = 1e4
N_GROUPS = 4
EXPERTS_PER_GROUP = 8
N_EXPERTS = 32
D_EXPERT = 512
ALPHA = (2.0 * DEPTH) ** 0.25
LANES = 128
VMEM_LIMIT = 48 * 1024 * 1024

SEG_SIZES = dict(qa=512, qb=512, ka=128, va=128, kbc=128, vbc=128, kbs=128, vbs=128,
                 kbw=128, vbw=128, gb=128, qc=1024, kc=1024, vc=1024)
SEG_ORDER = ("qa", "qb", "ka", "va", "kbc", "vbc", "kbs", "vbs", "kbw", "vbw", "gb", "qc", "kc", "vc")
SEG_OFF = {}
_o = 0
for _n in SEG_ORDER:
    SEG_OFF[_n] = _o
    _o += SEG_SIZES[_n]
PROJ_TN = 768
D_PROJ = -(-_o // PROJ_TN) * PROJ_TN
REF_SEGS = (("qa", 512), ("ka", 128), ("va", 128), ("qb", 512), ("kbc", 128), ("vbc", 128), ("kbs", 128),
            ("vbs", 128), ("kbw", 128), ("vbw", 128), ("gb", 24), ("qc", 1024), ("kc", 1024), ("vc", 1024))

SLOPES = [2.0 ** (-8.0 * (i + 1) / (HA + HB)) for i in range(HA + HB)]
STICK_EXIT = 105.0

ATT_TQ = 256
ATT_TK = 128
CMP_TQ = 128
STICK_T = 256
STICK_PAIRS = 2
STICK_FIRST = 2
M_INIT = -1e30


def _cparams(sem):
    return pltpu.CompilerParams(dimension_semantics=sem, vmem_limit_bytes=VMEM_LIMIT)


def _dot_nt(a, b):
    return lax.dot_general(a, b, (((1,), (1,)), ((), ())), preferred_element_type=F32)


def _dot_tn(a, b):
    return lax.dot_general(a, b, (((0,), (0,)), ((), ())), preferred_element_type=F32)


def _layer_norm(x, eps=1e-5):
    mu = jnp.mean(x, axis=-1, keepdims=True)
    xc = x - mu
    var = jnp.mean(xc * xc, axis=-1, keepdims=True)
    return xc * lax.rsqrt(var + eps)


def _ada_kernel(c_ref, w_ref, b_ref, o_ref):
    c = c_ref[...]
    cs = c * jax.nn.sigmoid(c)
    o_ref[0] = jnp.dot(cs, w_ref[0], preferred_element_type=F32, precision=HIGHEST) + b_ref[0]


def _ada_mod(c, w_ada, b_ada):
    B = c.shape[0]
    L, D, N6 = w_ada.shape
    cp = jnp.zeros((8, D), F32).at[:B].set(c)
    tn = 1024
    out = pl.pallas_call(
        _ada_kernel,
        grid=(L, N6 // tn),
        in_specs=[pl.BlockSpec((8, D), lambda l, j: (0, 0)),
                  pl.BlockSpec((1, D, tn), lambda l, j: (l, 0, j)),
                  pl.BlockSpec((1, 1, tn), lambda l, j: (l, 0, j))],
        out_specs=pl.BlockSpec((1, 8, tn), lambda l, j: (l, 0, j)),
        out_shape=jax.ShapeDtypeStruct((L, 8, N6), F32),
        compiler_params=_cparams(("arbitrary", "arbitrary")),
        name="ada_mod",
    )(cp, w_ada, b_ada.reshape(L, 1, N6))
    return out[:, :B]


def _inproj_kernel(x_ref, sc_ref, sh_ref, w_ref, o_ref, h_ref):
    @pl.when(pl.program_id(1) == 0)
    def _():
        h = _layer_norm(x_ref[...]) * (1.0 + sc_ref[0]) + sh_ref[0]
        h_ref[...] = h.astype(BF16)

    o_ref[...] = jnp.dot(h_ref[...], w_ref[...], preferred_element_type=F32).astype(BF16)


def _in_projection(x2d, sc, sh, w_packed, S):
    N, D = x2d.shape
    tm = 1024 if S % 1024 == 0 else S
    return pl.pallas_call(
        _inproj_kernel,
        grid=(N // tm, D_PROJ // PROJ_TN),
        in_specs=[pl.BlockSpec((tm, D), lambda i, j: (i, 0)),
                  pl.BlockSpec((1, 1, D), lambda i, j: (i * tm // S, 0, 0)),
                  pl.BlockSpec((1, 1, D), lambda i, j: (i * tm // S, 0, 0)),
                  pl.BlockSpec((D, PROJ_TN), lambda i, j: (0, j))],
        out_specs=pl.BlockSpec((tm, PROJ_TN), lambda i, j: (i, j)),
        out_shape=jax.ShapeDtypeStruct((N, D_PROJ), BF16),
        scratch_shapes=[pltpu.VMEM((tm, D), BF16)],
        compiler_params=_cparams(("arbitrary", "arbitrary")),
        name="ln_in_proj",
    )(x2d, sc, sh, w_packed)


def _pack_w_in(w_in_l):
    cols = {}
    off = 0
    for name, width in REF_SEGS:
        cols[name] = w_in_l[:, off:off + width]
        off += width
    parts = []
    for name in SEG_ORDER:
        wseg = cols[name]
        pad = SEG_SIZES[name] - wseg.shape[1]
        if pad:
            wseg = jnp.pad(wseg, ((0, 0), (0, pad)))
        parts.append(wseg)
    total = sum(SEG_SIZES.values())
    parts.append(jnp.zeros((w_in_l.shape[0], D_PROJ - total), w_in_l.dtype))
    return jnp.concatenate(parts, axis=1).astype(BF16)


def _compress_kernel(xa_ref, xb_ref, pe_ref, w1_ref, w2_ref, o_ref):
    for kv in range(2):
        xa = (xa_ref[kv, 0].astype(F32) + pe_ref[kv, 0:1, :]).astype(BF16)
        xb = (xb_ref[kv, 0].astype(F32) + pe_ref[kv, 1:2, :]).astype(BF16)
        hid = (jnp.dot(xa, w1_ref[kv, 0], preferred_element_type=F32)
               + jnp.dot(xb, w1_ref[kv, 1], preferred_element_type=F32))
        act = jax.nn.gelu(hid)
        o_ref[kv, 0] = jnp.dot(act.astype(BF16), w2_ref[kv], preferred_element_type=F32).astype(BF16)


def _compress(proj, pe_kv, w1_kv, w2_kv, B, S):
    nchunk = S // CMP_STRIDE
    ncp = nchunk
    width = CMP_STRIDE * KV_GROUPS * HEAD_DIM
    x16 = jnp.stack([proj[:, :, SEG_OFF[name]:SEG_OFF[name] + 128].reshape(B, nchunk, width)
                     for name in ("kbc", "vbc")])
    xb = jnp.concatenate([x16[:, :, 1:], jnp.zeros_like(x16[:, :, :1])], axis=2)
    eye = jnp.eye(KV_GROUPS, dtype=F32)
    w1 = jnp.einsum("kpldc,gh->kplgdhc", w1_kv.reshape(2, 2, CMP_STRIDE, HEAD_DIM, CMP_HIDDEN), eye)
    w1 = w1.reshape(2, 2, width, KV_GROUPS * CMP_HIDDEN).astype(BF16)
    w2 = jnp.einsum("kcd,gh->kgchd", w2_kv, eye).reshape(2, KV_GROUPS * CMP_HIDDEN, KV_GROUPS * HEAD_DIM).astype(BF16)
    pe = jnp.broadcast_to(pe_kv.reshape(2, 2, CMP_STRIDE, 1, HEAD_DIM),
                          (2, 2, CMP_STRIDE, KV_GROUPS, HEAD_DIM)).reshape(2, 2, width)
    blk = (2, 1, ncp, width)
    return pl.pallas_call(
        _compress_kernel,
        grid=(B,),
        in_specs=[pl.BlockSpec(blk, lambda b: (0, b, 0, 0)),
                  pl.BlockSpec(blk, lambda b: (0, b, 0, 0)),
                  pl.BlockSpec((2, 2, width), lambda b: (0, 0, 0)),
                  pl.BlockSpec((2, 2, width, KV_GROUPS * CMP_HIDDEN), lambda b: (0, 0, 0, 0)),
                  pl.BlockSpec((2, KV_GROUPS * CMP_HIDDEN, KV_GROUPS * HEAD_DIM), lambda b: (0, 0, 0))],
        out_specs=pl.BlockSpec((2, 1, ncp, KV_GROUPS * HEAD_DIM), lambda b: (0, b, 0, 0)),
        out_shape=jax.ShapeDtypeStruct((2, B, ncp, KV_GROUPS * HEAD_DIM), BF16),
        compiler_params=_cparams(("arbitrary",)),
        name="nsa_compress",
    )(x16, xb, pe, w1, w2)


def _stack_queries(q, tq):
    parts = []
    for h in range(N_STACK):
        q_h = q[:, h * HEAD_DIM:(h + 1) * HEAD_DIM]
        z = jnp.zeros_like(q_h)
        parts.append(jnp.concatenate([q_h, z] if h // REP == 0 else [z, q_h], axis=1))
    return jnp.concatenate(parts, axis=0) * SCALE


def _query_key_offsets(tk, tq):
    key = lax.broadcasted_iota(jnp.int32, (tk, N_STACK * tq), 0)
    qry = lax.broadcasted_iota(jnp.int32, (tk, N_STACK * tq), 1) & (tq - 1)
    return qry - key


def _flash_step(s, m, l, acc, v_t):
    m_new = jnp.maximum(m, jnp.max(s, axis=0, keepdims=True))
    a = jnp.exp(m - m_new)
    p = jnp.exp(s - m_new)
    l_new = a * l + jnp.sum(p, axis=0, keepdims=True)
    return m_new, l_new, a * acc + _dot_tn(v_t, p.astype(BF16))


def _store_heads(acc_n, gates_t, gate_col, o_ref, tq):
    for pair in range(N_STACK // 2):
        g = (2 * pair) // REP
        blocks = []
        for h in (2 * pair, 2 * pair + 1):
            blk = acc_n[g * HEAD_DIM:(g + 1) * HEAD_DIM, h * tq:(h + 1) * tq]
            if gates_t is not None:
                c = h * 3 + gate_col
                blk = blk * gates_t[c:c + 1, :]
            blocks.append(blk)
        o_ref[0, :, pair * LANES:(pair + 1) * LANES] = jnp.concatenate(blocks, axis=0).T


def _head_rows(values, tq):
    return jnp.repeat(jnp.asarray(values, F32), tq).reshape(1, N_STACK * tq)


def _band_kernel(*refs, window, use_sink, gate_col, tq, tk):
    if use_sink:
        slope_ref, sink_ref, q_ref, k_ref, v_ref, o_ref = refs
        g_ref = None
    else:
        slope_ref, q_ref, k_ref, v_ref, g_ref, o_ref = refs
    i = pl.program_id(1)
    n = N_STACK * tq
    per_q = tq // tk
    q_stack = _stack_queries(q_ref[0], tq)
    rel = _query_key_offsets(tk, tq)
    slope = slope_ref[...]
    bias0 = slope * rel.astype(F32)
    if use_sink:
        m, l = sink_ref[...], jnp.ones((1, n), F32)
    else:
        m, l = jnp.full((1, n), M_INIT, F32), jnp.zeros((1, n), F32)
    acc = jnp.zeros((LANES, n), F32)
    last = (i + 1) * per_q - 1
    for j in range(per_q + window // tk):
        c = j * tk - (tq - tk)
        start = pl.multiple_of(jnp.maximum(last - j, 0) * tk, tk)
        k_t = k_ref[0, pl.ds(start, tk), :]
        v_t = v_ref[0, pl.ds(start, tk), :]
        s = _dot_nt(k_t, q_stack) - (bias0 + slope * float(c))
        keep = None
        if c < tk - 1:
            keep = rel >= -c
        if c + tq - 1 >= window:
            far = rel < window - c
            keep = far if keep is None else keep & far
        if keep is not None:
            s = jnp.where(keep, s, -jnp.inf)
        if j >= per_q:
            s = jnp.where(last >= j, s, -jnp.inf)
        m, l, acc = _flash_step(s, m, l, acc, v_t)
    gates_t = None if g_ref is None else jax.nn.sigmoid(g_ref[0].astype(F32)).T
    _store_heads(acc * (1.0 / l), gates_t, gate_col, o_ref, tq)


def _band_attention(proj, qname, kname, vname, window, slopes, sinks, gate_col, B, S):
    tq, tk = ATT_TQ, ATT_TK
    n = N_STACK * tq
    kern = functools.partial(_band_kernel, window=window, use_sink=sinks is not None, gate_col=gate_col, tq=tq, tk=tk)
    qb, kb, vb, gbk = SEG_OFF[qname] // 512, SEG_OFF[kname] // 128, SEG_OFF[vname] // 128, SEG_OFF["gb"] // 128
    row = pl.BlockSpec((1, n), lambda b, i: (0, 0))
    specs = [pl.BlockSpec((1, tq, 512), lambda b, i: (b, i, qb)),
             pl.BlockSpec((1, S, 128), lambda b, i: (b, 0, kb)),
             pl.BlockSpec((1, S, 128), lambda b, i: (b, 0, vb))]
    args = [proj, proj, proj]
    if sinks is not None:
        specs = [row, row] + specs
        args = [_head_rows(slopes, tq), _head_rows(sinks, tq)] + args
    else:
        specs = [row] + specs + [pl.BlockSpec((1, tq, 128), lambda b, i: (b, i, gbk))]
        args = [_head_rows(slopes, tq)] + args + [proj]
    return pl.pallas_call(
        kern,
        grid=(B, S // tq),
        in_specs=specs,
        out_specs=pl.BlockSpec((1, tq, 512), lambda b, i: (b, i, 0)),
        out_shape=jax.ShapeDtypeStruct((B, S, 512), F32),
        compiler_params=_cparams(("arbitrary", "arbitrary")),
        name="band_attn_w%d" % window,
    )(*args)


def _cmp_select_kernel(slope_ref, q_ref, kc_ref, vc_ref, g_ref, ovt_ref, o_ref, sel_ref, cnt_ref, *, tq, n_pick):
    i = pl.program_id(1)
    ncp = kc_ref.shape[2]
    n = N_STACK * tq
    q_stack = _stack_queries(q_ref[0], tq)
    t_row = i * tq + (lax.broadcasted_iota(jnp.int32, (1, n), 1) & (tq - 1))
    n_idx = lax.broadcasted_iota(jnp.int32, (ncp, n), 0)
    last_visible = (t_row - (CMP_BLOCK - 1)) >> (CMP_STRIDE.bit_length() - 1)
    s = _dot_nt(kc_ref[0, 0], q_stack) + slope_ref[...]
    s = jnp.where(n_idx <= last_visible, s, -jnp.inf)
    m = jnp.max(s, axis=0, keepdims=True)
    m = jnp.where(m == -jnp.inf, 0.0, m)
    e = jnp.exp(s - m)
    den = jnp.sum(e, axis=0, keepdims=True)
    p = e * (1.0 / jnp.maximum(den, 1e-30))
    gates_t = jax.nn.sigmoid(g_ref[0].astype(F32)).T
    _store_heads(_dot_tn(vc_ref[0, 0], p.astype(BF16)), gates_t, 0, o_ref, tq)
    psum = jnp.concatenate(
        [sum(p[:, (g * REP + r) * tq:(g * REP + r + 1) * tq] for r in range(REP)) for g in range(KV_GROUPS)], axis=1)
    p_hi = psum.astype(BF16)
    p_lo = (psum - p_hi.astype(F32)).astype(BF16)
    imp = (jnp.dot(ovt_ref[...], p_hi, preferred_element_type=F32)
           + jnp.dot(ovt_ref[...], p_lo, preferred_element_type=F32))
    w = KV_GROUPS * tq
    j_idx = lax.broadcasted_iota(jnp.int32, (LANES, w), 0)
    t_sel = i * tq + (lax.broadcasted_iota(jnp.int32, (1, w), 1) & (tq - 1))
    cur = t_sel // SEL_BLOCK
    forced = (j_idx == 0) | (j_idx == cur) | (j_idx == cur - 1)
    valid = j_idx * SEL_BLOCK <= t_sel
    imp = jnp.where(valid, jnp.where(forced, FORCE_SCORE, imp), -1.0)

    def pick(_, carry):
        imp, sel = carry
        mx = jnp.max(imp, axis=0, keepdims=True)
        first = jnp.min(jnp.where(imp == mx, j_idx, LANES), axis=0, keepdims=True)
        hit = j_idx == first
        return jnp.where(hit, -2.0, imp), jnp.where(hit, 1.0, sel)

    _, sel = lax.fori_loop(0, n_pick, pick, (imp, jnp.zeros((LANES, w), F32)))
    sel_b = sel.astype(BF16)
    for g in range(KV_GROUPS):
        sel_ref[0, g] = sel_b[:, g * tq:(g + 1) * tq]
    cnt = _dot_nt(jnp.ones((8, w), BF16), sel_b)
    cnt_ref[0, 0] = cnt[0:1, :]


def _overlap_matrix_t(ncp):
    n = np.arange(ncp)[None, :]
    j = np.arange(LANES)[:, None]
    start = n * CMP_STRIDE
    end = start + CMP_BLOCK - 1
    return ((end >= j * SEL_BLOCK) & (start < j * SEL_BLOCK + SEL_BLOCK)).astype(np.float32)


def _cmp_select(proj, kvc, B, S):
    tq = CMP_TQ
    nq = S // tq
    ncp = kvc.shape[2]
    n = N_STACK * tq
    n_pick = min(N_SELECT, S // SEL_BLOCK)
    qb, gbk = SEG_OFF["qb"] // 512, SEG_OFF["gb"] // 128
    ovt = jnp.asarray(_overlap_matrix_t(ncp), BF16)
    block_end = jnp.arange(ncp, dtype=F32).reshape(ncp, 1) * CMP_STRIDE + (CMP_BLOCK - 1)
    key_bias = block_end * _head_rows(SLOPES[HA:], tq)
    kern = functools.partial(_cmp_select_kernel, tq=tq, n_pick=n_pick)
    kv_blk = (1, 1, ncp, LANES)
    return pl.pallas_call(
        kern,
        grid=(B, nq),
        in_specs=[pl.BlockSpec((ncp, n), lambda b, i: (0, 0)),
                  pl.BlockSpec((1, tq, 512), lambda b, i: (b, i, qb)),
                  pl.BlockSpec(kv_blk, lambda b, i: (0, b, 0, 0)),
                  pl.BlockSpec(kv_blk, lambda b, i: (1, b, 0, 0)),
                  pl.BlockSpec((1, tq, 128), lambda b, i: (b, i, gbk)),
                  pl.BlockSpec((LANES, ncp), lambda b, i: (0, 0))],
        out_specs=[pl.BlockSpec((1, tq, 512), lambda b, i: (b, i, 0)),
                   pl.BlockSpec((1, KV_GROUPS, LANES, tq), lambda b, i: (b, 0, 0, i)),
                   pl.BlockSpec((1, 1, 1, LANES), lambda b, i: (b, i, 0, 0))],
        out_shape=[jax.ShapeDtypeStruct((B, S, 512), F32),
                   jax.ShapeDtypeStruct((B, KV_GROUPS, LANES, S), BF16),
                   jax.ShapeDtypeStruct((B, nq, 1, LANES), F32)],
        compiler_params=_cparams(("arbitrary", "arbitrary")),
        name="nsa_cmp_select",
    )(key_bias, proj, kvc, kvc, proj, ovt)


def _sel_kernel(flag_ref, slope_ref, q_ref, k_ref, v_ref, sel_ref, g_ref, o_ref, m_s, l_s, acc_s, *, tq, tk, nq):
    b = pl.program_id(0)
    i = pl.program_id(1)
    nk = nq * (tq // tk)
    per_q = tq // tk
    q_stack = _stack_queries(q_ref[0], tq)
    rel = _query_key_offsets(tk, tq)
    slope = slope_ref[...]
    bias0 = slope * rel.astype(F32)
    per_tile = tk // SEL_BLOCK
    key_blk = lax.broadcasted_iota(jnp.int32, (tk, LANES), 0) // SEL_BLOCK
    blk_id = lax.broadcasted_iota(jnp.int32, (tk, LANES), 1)
    m_s[...] = jnp.full(m_s.shape, M_INIT, F32)
    l_s[...] = jnp.zeros(l_s.shape, F32)
    acc_s[...] = jnp.zeros(acc_s.shape, F32)

    def tile(kt, diagonal):
        start = pl.multiple_of(kt * tk, tk)
        k_t = k_ref[0, pl.ds(start, tk), :]
        v_t = v_ref[0, pl.ds(start, tk), :]
        expand = jnp.where(blk_id == kt * per_tile + key_blk, 1.0, 0.0).astype(BF16)
        picked = [jnp.dot(expand, sel_ref[0, g], preferred_element_type=F32) for g in range(KV_GROUPS)]
        keep = jnp.concatenate([picked[h // REP] for h in range(N_STACK)], axis=1) > 0.5
        shift = i * tq - kt * tk
        if diagonal:
            keep = keep & (rel >= -shift)
        s = _dot_nt(k_t, q_stack) - (bias0 + slope * shift.astype(F32))
        s = jnp.where(keep, s, -jnp.inf)
        m, l, acc = _flash_step(s, m_s[...], l_s[...], acc_s[...], v_t)
        m_s[...] = m
        l_s[...] = l
        acc_s[...] = acc

    for d in range(per_q):
        tile((i + 1) * per_q - 1 - d, True)

    def body(j, _):
        kt = i * per_q - 1 - j

        @pl.when(flag_ref[(b * nq + i) * nk + kt] > 0)
        def _():
            tile(kt, False)

        return 0

    lax.fori_loop(0, i * per_q, body, 0)
    gates_t = jax.nn.sigmoid(g_ref[0].astype(F32)).T
    _store_heads(acc_s[...] * (1.0 / l_s[...]), gates_t, 1, o_ref, tq)


def _selected_attention(proj, sel, flags, B, S):
    tq, tk = ATT_TQ, ATT_TK
    nq = S // tq
    n = N_STACK * tq
    qb, kb, vb, gbk = SEG_OFF["qb"] // 512, SEG_OFF["kbs"] // 128, SEG_OFF["vbs"] // 128, SEG_OFF["gb"] // 128
    kern = functools.partial(_sel_kernel, tq=tq, tk=tk, nq=nq)
    grid_spec = pltpu.PrefetchScalarGridSpec(
        num_scalar_prefetch=1,
        grid=(B, nq),
        in_specs=[pl.BlockSpec((1, n), lambda b, i, f: (0, 0)),
                  pl.BlockSpec((1, tq, 512), lambda b, i, f: (b, i, qb)),
                  pl.BlockSpec((1, S, 128), lambda b, i, f: (b, 0, kb)),
                  pl.BlockSpec((1, S, 128), lambda b, i, f: (b, 0, vb)),
                  pl.BlockSpec((1, KV_GROUPS, LANES, tq), lambda b, i, f: (b, 0, 0, i)),
                  pl.BlockSpec((1, tq, 128), lambda b, i, f: (b, i, gbk))],
        out_specs=pl.BlockSpec((1, tq, 512), lambda b, i, f: (b, i, 0)),
        scratch_shapes=[pltpu.VMEM((1, n), F32), pltpu.VMEM((1, n), F32), pltpu.VMEM((LANES, n), F32)],
    )
    return pl.pallas_call(
        kern,
        grid_spec=grid_spec,
        out_shape=jax.ShapeDtypeStruct((B, S, 512), F32),
        compiler_params=_cparams(("arbitrary", "arbitrary")),
        name="nsa_selected",
    )(flags, _head_rows(SLOPES[HA:], tq), proj, proj, proj, sel, proj)


def _selection_flags(cnt, B, S):
    nq, nk = S // ATT_TQ, S // ATT_TK
    per_tile = ATT_TK // SEL_BLOCK
    c = cnt[:, :, 0, :nk * per_tile].reshape(B, nq, ATT_TQ // CMP_TQ, nk, per_tile).sum((2, 4))
    return (c > 0).astype(jnp.int32).reshape(-1)


def _stick_kernel(*refs, t, pairs):
    o_ref = refs[3 * pairs]
    i = pl.program_id(2)
    started = [_stick_first_tiles(i, refs[p], refs[pairs + p], refs[2 * pairs + p], t) for p in range(pairs)]
    for p, (carry, acc, step) in enumerate(started):
        def cond(state):
            kt, go, _, _ = state
            return (kt >= 0) & (go > 0)

        def body(state, step=step):
            kt, _, carry, acc = state
            carry, acc = step(kt, carry, acc)
            return kt - 1, (jnp.max(carry) > -STICK_EXIT).astype(jnp.int32), carry, acc

        go0 = (jnp.max(carry) > -STICK_EXIT).astype(jnp.int32)
        _, _, _, acc = lax.while_loop(cond, body, (i - STICK_FIRST, go0, carry, acc))
        o_ref[0, :, p * LANES:(p + 1) * LANES] = jnp.concatenate([acc[:HEAD_DIM, :t], acc[HEAD_DIM:, t:]], axis=0).T


def _stick_first_tiles(i, q_ref, k_ref, v_ref, t):
    n = 2 * t
    lane = lax.broadcasted_iota(jnp.int32, (t, LANES), 1)
    q = q_ref[0]
    zero = jnp.zeros_like(q)
    q_stack = jnp.concatenate([jnp.where(lane < HEAD_DIM, q, zero), jnp.where(lane >= HEAD_DIM, q, zero)], axis=0) * SCALE
    key = lax.broadcasted_iota(jnp.int32, (t, n), 0)
    qry = lax.broadcasted_iota(jnp.int32, (t, n), 1) & (t - 1)
    causal = key < qry
    r_i = lax.broadcasted_iota(jnp.int32, (t, t), 0)
    c_i = lax.broadcasted_iota(jnp.int32, (t, t), 1)
    suffix = jnp.where(c_i >= r_i, 1.0, 0.0).astype(BF16)
    suffix2 = jnp.concatenate([suffix, suffix], axis=1)

    def scores(kt, masked):
        start = pl.multiple_of(kt * t, t)
        z = _dot_nt(k_ref[0, pl.ds(start, t), :], q_stack)
        log_1mb = -(jnp.maximum(z, 0.0) + jnp.log(1.0 + jnp.exp(-jnp.abs(z))))
        if masked:
            log_1mb = jnp.where(causal, log_1mb, 0.0)
        hi = log_1mb.astype(BF16)
        lo = (log_1mb - hi.astype(F32)).astype(BF16)
        inc = jnp.dot(suffix2, jnp.concatenate([hi, lo], axis=0),
                      preferred_element_type=F32)
        return z, inc

    def weights(z, inc, carry):
        return jnp.exp(z + inc + carry)

    def values(kt, a):
        start = pl.multiple_of(kt * t, t)
        return _dot_tn(v_ref[0, pl.ds(start, t), :], a.astype(BF16))

    z0, inc0 = scores(i, True)
    before = [(d, jnp.maximum(i - d, 0)) for d in range(1, STICK_FIRST)]
    ahead = [scores(kt, False) for _, kt in before]
    carry = inc0[0:1, :]
    acc = values(i, jnp.where(causal, weights(z0, inc0, 0.0), 0.0))
    for (d, kt), (z_d, inc_d) in zip(before, ahead):
        acc = acc + values(kt, jnp.where(i >= d, weights(z_d, inc_d, carry), 0.0))
        carry = carry + jnp.where(i >= d, inc_d[0:1, :], 0.0)

    def step(kt, carry, acc):
        z, inc = scores(kt, False)
        return carry + inc[0:1, :], acc + values(kt, weights(z, inc, carry))

    return carry, acc, step


def _stick_attention(proj, B, S):
    t = min(STICK_T, S)
    pairs = STICK_PAIRS
    qb, kb, vb = SEG_OFF["qc"] // 128, SEG_OFF["kc"] // 128, SEG_OFF["vc"] // 128
    col = lambda base, p: (lambda b, g, i: (b, i, base + g * pairs + p))
    whole = lambda base, p: (lambda b, g, i: (b, 0, base + g * pairs + p))
    specs = ([pl.BlockSpec((1, t, 128), col(qb, p)) for p in range(pairs)]
             + [pl.BlockSpec((1, S, 128), whole(kb, p)) for p in range(pairs)]
             + [pl.BlockSpec((1, S, 128), whole(vb, p)) for p in range(pairs)])
    return pl.pallas_call(
        functools.partial(_stick_kernel, t=t, pairs=pairs),
        grid=(B, HC // 2 // pairs, S // t),
        in_specs=specs,
        out_specs=pl.BlockSpec((1, t, pairs * LANES), lambda b, g, i: (b, i, g)),
        out_shape=jax.ShapeDtypeStruct((B, S, HC * HEAD_DIM), F32),
        compiler_params=_cparams(("arbitrary", "arbitrary", "arbitrary")),
        name="stick_breaking",
    )(*([proj] * (3 * pairs)))


def _group_rms(o, eps=1e-6):
    return o * lax.rsqrt(jnp.mean(o * o, axis=-1, keepdims=True) + eps)


def _outproj_kernel(oa_ref, ob1_ref, ob2_ref, ob3_ref, oc_ref, x_ref, gain_ref, w_ref, g1_ref, lg_ref, lb_ref,
                    sc2_ref, sh2_ref, wr_ref, br_ref, x1_ref, h2_ref, route_ref):
    o_b = ob1_ref[0] + ob2_ref[0] + ob3_ref[0]
    merged = jnp.concatenate([_group_rms(oa_ref[0]), _group_rms(o_b), _group_rms(oc_ref[0])], axis=-1)
    merged = (merged * gain_ref[...]).astype(BF16)
    y = jnp.dot(merged, w_ref[...], preferred_element_type=F32)
    x1 = _layer_norm(ALPHA * x_ref[0] + (1.0 + g1_ref[0]) * y) * lg_ref[...] + lb_ref[...]
    x1_ref[0] = x1
    h2 = _layer_norm(x1) * (1.0 + sc2_ref[0]) + sh2_ref[0]
    h2_ref[0] = h2
    h_hi = h2.astype(BF16)
    h_lo = (h2 - h_hi.astype(F32)).astype(BF16)
    part = jnp.dot(h_hi, wr_ref[...], preferred_element_type=F32)
    logits = (part[:, :LANES] + part[:, LANES:]
              + jnp.dot(h_lo, wr_ref[:, :LANES], preferred_element_type=F32) + br_ref[...])
    tm = logits.shape[0]
    lane = lax.broadcasted_iota(jnp.int32, (tm, LANES), 1)
    lg = jnp.where(lane < N_GROUPS, logits, -jnp.inf)
    mg = jnp.max(lg, axis=-1, keepdims=True)
    p_group = 1.0 / jnp.sum(jnp.exp(lg - mg), axis=-1, keepdims=True)
    g_sel = jnp.min(jnp.where(lg == mg, lane, LANES), axis=-1, keepdims=True)
    e_idx = lane - N_GROUPS
    in_grp = (e_idx >= g_sel * EXPERTS_PER_GROUP) & (e_idx < (g_sel + 1) * EXPERTS_PER_GROUP)
    le = jnp.where(in_grp, logits, -jnp.inf)
    m1 = jnp.max(le, axis=-1, keepdims=True)
    i1 = jnp.min(jnp.where(le == m1, e_idx, LANES), axis=-1, keepdims=True)
    le2 = jnp.where(e_idx == i1, -jnp.inf, le)
    m2 = jnp.max(le2, axis=-1, keepdims=True)
    i2 = jnp.min(jnp.where(le2 == m2, e_idx, LANES), axis=-1, keepdims=True)
    den = jnp.sum(jnp.exp(le - m1), axis=-1, keepdims=True)
    p1 = 1.0 / den
    p2 = jnp.exp(m2 - m1) / den
    gate1 = p_group * p1 / (p1 + p2)
    gate2 = p_group * p2 / (p1 + p2)
    route = jnp.where(lane == 0, i1.astype(F32),
                      jnp.where(lane == 1, i2.astype(F32),
                                jnp.where(lane == 2, gate1, jnp.where(lane == 3, gate2, 0.0))))
    route_ref[0] = route


def _out_projection(o_a, o_b1, o_b2, o_b3, o_c, x, gain, w_out, g1, ln_g, ln_b, sc2, sh2, w_r, b_r, B, S):
    D = D_MODEL
    tm = 256
    row = lambda b, i: (b, i, 0)
    vec = lambda b, i: (0, 0)
    bvec = lambda b, i: (b, 0, 0)
    return pl.pallas_call(
        _outproj_kernel,
        grid=(B, S // tm),
        in_specs=[pl.BlockSpec((1, tm, 512), row), pl.BlockSpec((1, tm, 512), row),
                  pl.BlockSpec((1, tm, 512), row), pl.BlockSpec((1, tm, 512), row),
                  pl.BlockSpec((1, tm, 1024), row), pl.BlockSpec((1, tm, D), row),
                  pl.BlockSpec((1, D), vec), pl.BlockSpec((D, D), vec),
                  pl.BlockSpec((1, 1, D), bvec), pl.BlockSpec((1, D), vec), pl.BlockSpec((1, D), vec),
                  pl.BlockSpec((1, 1, D), bvec), pl.BlockSpec((1, 1, D), bvec),
                  pl.BlockSpec((D, 2 * LANES), vec), pl.BlockSpec((1, LANES), vec)],
        out_specs=[pl.BlockSpec((1, tm, D), row), pl.BlockSpec((1, tm, D), row),
                   pl.BlockSpec((1, tm, LANES), row)],
        out_shape=[jax.ShapeDtypeStruct((B, S, D), F32), jax.ShapeDtypeStruct((B, S, D), F32),
                   jax.ShapeDtypeStruct((B, S, LANES), F32)],
        compiler_params=_cparams(("arbitrary", "arbitrary")),
        name="out_proj_norm_route",
    )(o_a, o_b1, o_b2, o_b3, o_c, x, gain, w_out, g1, ln_g, ln_b, sc2, sh2, w_r, b_r)


ROW_UNROLL = 8


def _row_copy(src_ref, src_row, dst_ref, dst_row, sem):
    return pltpu.make_async_copy(src_ref.at[pl.ds(src_row, 1)], dst_ref.at[pl.ds(dst_row, 1)], sem)


def _dispatch_kernel(dest_ref, pend_ref, h_ref, xs_ref, zero_s, sem, zsem, *, tm, bm):
    i = pl.program_id(0)

    @pl.when(i == 0)
    def _():
        zero_s[...] = jnp.zeros(zero_s.shape, F32)

        def zero_copy(e):
            start = pl.multiple_of(jnp.maximum(pend_ref[e] - bm, 0), bm)
            return pltpu.make_async_copy(zero_s, xs_ref.at[pl.ds(start, bm)], zsem)

        def has_rows(e):
            return pend_ref[e] > (pend_ref[e - 1] if e else 0)

        for e in range(N_EXPERTS):
            pl.when(has_rows(e))(lambda e=e: zero_copy(e).start())
        for e in range(N_EXPERTS):
            pl.when(has_rows(e))(lambda e=e: zero_copy(e).wait())

        def tail_copy(blk):
            return pltpu.make_async_copy(zero_s, xs_ref.at[pl.ds(pl.multiple_of(blk * bm, bm), bm)], zsem)

        def tail_start(blk, carry):
            tail_copy(blk).start()
            return carry

        def tail_wait(blk, carry):
            tail_copy(blk).wait()
            return carry

        first_free = pend_ref[N_EXPERTS - 1] // bm
        lax.fori_loop(first_free, xs_ref.shape[0] // bm, tail_start, 0)
        lax.fori_loop(first_free, xs_ref.shape[0] // bm, tail_wait, 0)

    def body(blk, _):
        for u in range(ROW_UNROLL):
            r = blk * ROW_UNROLL + u
            for k in range(2):
                _row_copy(h_ref, r, xs_ref, dest_ref[(i * tm + r) * 2 + k], sem).start()
        return 0

    lax.fori_loop(0, tm // ROW_UNROLL, body, 0)
    for k in range(2):
        pltpu.make_async_copy(h_ref, xs_ref.at[pl.ds(0, tm)], sem).wait()


def _dispatch(dest, pad_end, h2, rows, bm):
    N, D = h2.shape
    tm = MOE_TM
    grid_spec = pltpu.PrefetchScalarGridSpec(
        num_scalar_prefetch=2,
        grid=(N // tm,),
        in_specs=[pl.BlockSpec((tm, D), lambda i, d, p: (i, 0))],
        out_specs=pl.BlockSpec(memory_space=pl.ANY),
        scratch_shapes=[pltpu.VMEM((bm, D), F32), pltpu.SemaphoreType.DMA(()), pltpu.SemaphoreType.DMA(())],
    )
    return pl.pallas_call(
        functools.partial(_dispatch_kernel, tm=tm, bm=bm),
        grid_spec=grid_spec,
        out_shape=jax.ShapeDtypeStruct((rows, D), F32),
        compiler_params=_cparams(("arbitrary",)),
        name="moe_dispatch",
    )(dest, pad_end, h2)


def _expert_kernel(be_ref, nb_ref, x_ref, wg_ref, wu_ref, wd_ref, y_ref, wg_s, wu_s, wd_s):
    i = pl.program_id(0)

    @pl.when(i < nb_ref[0])
    def _():
        @pl.when((i == 0) | (be_ref[i] != be_ref[jnp.maximum(i - 1, 0)]))
        def _():
            wg_s[...] = wg_ref[0, 0].astype(BF16)
            wu_s[...] = wu_ref[0, 0].astype(BF16)
            wd_s[...] = wd_ref[0, 0].astype(BF16)

        x = x_ref[...].astype(BF16)
        hg = jnp.dot(x, wg_s[...], preferred_element_type=F32)
        hu = jnp.dot(x, wu_s[...], preferred_element_type=F32)
        hid = (hg * jax.nn.sigmoid(hg)) * hu
        y_ref[...] = jnp.dot(hid.astype(BF16), wd_s[...], preferred_element_type=F32)

    @pl.when(i >= nb_ref[0])
    def _():
        y_ref[...] = jnp.zeros(y_ref.shape, F32)


def _expert_mlp(blk_expert, n_used, xs, w_gate, w_up, w_down, layer, bm):
    rows, D = xs.shape
    nblk = rows // bm
    blk = lambda i, be, nb: (jnp.minimum(i, nb[0] - 1), 0)
    wmap = lambda i, be, nb: (layer, be[jnp.minimum(i, nb[0] - 1)], 0, 0)
    grid_spec = pltpu.PrefetchScalarGridSpec(
        num_scalar_prefetch=2,
        grid=(nblk,),
        in_specs=[pl.BlockSpec((bm, D), blk),
                  pl.BlockSpec((1, 1, D, D_EXPERT), wmap),
                  pl.BlockSpec((1, 1, D, D_EXPERT), wmap),
                  pl.BlockSpec((1, 1, D_EXPERT, D), wmap)],
        out_specs=pl.BlockSpec((bm, D), lambda i, be, nb: (i, 0)),
        scratch_shapes=[pltpu.VMEM((D, D_EXPERT), BF16), pltpu.VMEM((D, D_EXPERT), BF16),
                        pltpu.VMEM((D_EXPERT, D), BF16)],
    )
    return pl.pallas_call(
        _expert_kernel,
        grid_spec=grid_spec,
        out_shape=jax.ShapeDtypeStruct((rows, D), F32),
        compiler_params=_cparams(("arbitrary",)),
        name="expert_mlp",
    )(blk_expert, n_used, xs, w_gate, w_up, w_down)


def _combine_kernel(dest_ref, x_ref, route_ref, g2_ref, lg_ref, lb_ref, yb_ref, o_ref, buf, sem, *, tm):
    i = pl.program_id(0)

    def body(blk, _):
        for u in range(ROW_UNROLL):
            r = blk * ROW_UNROLL + u
            for k in range(2):
                _row_copy(yb_ref, dest_ref[(i * tm + r) * 2 + k], buf.at[k], r, sem).start()
        return 0

    lax.fori_loop(0, tm // ROW_UNROLL, body, 0)
    for k in range(2):
        pltpu.make_async_copy(yb_ref.at[pl.ds(0, tm)], buf.at[k], sem).wait()
    route = route_ref[...]
    y = buf[0] * route[:, 2:3] + buf[1] * route[:, 3:4]
    o_ref[...] = _layer_norm(ALPHA * x_ref[...] + (1.0 + g2_ref[0]) * y) * lg_ref[...] + lb_ref[...]


def _combine_post_norm(dest, x1, route, g2, ln_g, ln_b, yb, S):
    N, D = x1.shape
    tm = MOE_TM
    grid_spec = pltpu.PrefetchScalarGridSpec(
        num_scalar_prefetch=1,
        grid=(N // tm,),
        in_specs=[pl.BlockSpec((tm, D), lambda i, d: (i, 0)),
                  pl.BlockSpec((tm, LANES), lambda i, d: (i, 0)),
                  pl.BlockSpec((1, 1, D), lambda i, d: (i * tm // S, 0, 0)),
                  pl.BlockSpec((1, D), lambda i, d: (0, 0)), pl.BlockSpec((1, D), lambda i, d: (0, 0)),
                  pl.BlockSpec(memory_space=pl.ANY)],
        out_specs=pl.BlockSpec((tm, D), lambda i, d: (i, 0)),
        scratch_shapes=[pltpu.VMEM((2, tm, D), F32), pltpu.SemaphoreType.DMA(())],
    )
    return pl.pallas_call(
        functools.partial(_combine_kernel, tm=tm),
        grid_spec=grid_spec,
        out_shape=jax.ShapeDtypeStruct((N, D), F32),
        compiler_params=_cparams(("arbitrary",)),
        name="moe_combine_post_norm",
    )(dest, x1, route, g2, ln_g, ln_b, yb)


MOE_BM = 256
MOE_TM = 512


def _moe(h2, x1, route, g2, ln_g, ln_b, w_gate, w_up, w_down, layer, B, S):
    D = D_MODEL
    N = B * S
    A = 2 * N
    bm = MOE_BM
    r = route.reshape(N, LANES)
    e_flat = r[:, 0:2].astype(jnp.int32).reshape(A)
    onehot = (e_flat[:, None] == jnp.arange(N_EXPERTS)[None, :]).astype(jnp.int32)
    csum = jnp.cumsum(onehot, axis=0)
    counts = csum[-1]
    rank = jnp.sum((csum - onehot) * onehot, axis=1)
    padded = (counts + bm - 1) // bm * bm
    pad_end = jnp.cumsum(padded).astype(jnp.int32)
    pad_start = pad_end - padded
    dest = (pad_start[e_flat] + rank).astype(jnp.int32)
    nblk = -(-A // bm) + N_EXPERTS
    blk_start = jnp.arange(nblk) * bm
    blk_expert = jnp.minimum(jnp.sum(blk_start[:, None] >= pad_end[None, :], axis=1), N_EXPERTS - 1).astype(jnp.int32)
    n_used = (pad_end[-1] // bm).astype(jnp.int32).reshape(1)
    xs = _dispatch(dest, pad_end, h2.reshape(N, D), nblk * bm, bm)
    yb = _expert_mlp(blk_expert, n_used, xs, w_gate, w_up, w_down, layer, bm)
    out = _combine_post_norm(dest, x1.reshape(N, D), r, g2, ln_g, ln_b, yb, S)
    return out.reshape(B, S, D)


def kernel(x, c, w_ada, b_ada, w_in, sinks, cmp_pe_k, cmp_w1_k, cmp_w2_k, cmp_pe_v, cmp_w1_v, cmp_w2_v, mix_gain, w_out, ln1_g, ln1_b, w_rg, b_rg, w_re, b_re, w_gate, w_up, w_down, ln2_g, ln2_b):
    B, S, D = x.shape
    L = w_ada.shape[0]
    mod = _ada_mod(c, w_ada, b_ada)
    for l in range(L):
        sh1, sc1, g1, sh2, sc2, g2 = [mod[l, :, k * D:(k + 1) * D].reshape(B, 1, D) for k in range(6)]
        proj = _in_projection(x.reshape(B * S, D), sc1, sh1, _pack_w_in(w_in[l]), S).reshape(B, S, D_PROJ)
        o_a = _band_attention(proj, "qa", "ka", "va", WINDOW_A, SLOPES[:HA], sinks[l], 0, B, S)
        kvc = _compress(proj, jnp.stack([cmp_pe_k[l], cmp_pe_v[l]]), jnp.stack([cmp_w1_k[l], cmp_w1_v[l]]),
                        jnp.stack([cmp_w2_k[l], cmp_w2_v[l]]), B, S)
        o_cmp, sel, cnt = _cmp_select(proj, kvc, B, S)
        o_slc = _selected_attention(proj, sel, _selection_flags(cnt, B, S), B, S)
        o_win = _band_attention(proj, "qb", "kbw", "vbw", WINDOW_B, SLOPES[HA:], None, 2, B, S)
        o_c = _stick_attention(proj, B, S)
        w_r = jnp.concatenate([w_rg[l], w_re[l].transpose(1, 0, 2).reshape(D, N_EXPERTS),
                               jnp.zeros((D, LANES - N_GROUPS - N_EXPERTS), F32)], axis=1)
        b_r = jnp.concatenate([b_rg[l], b_re[l].reshape(N_EXPERTS),
                               jnp.zeros((LANES - N_GROUPS - N_EXPERTS,), F32)]).reshape(1, LANES)
        w_r_hi = w_r.astype(BF16)
        w_r_lo = (w_r - w_r_hi.astype(F32)).astype(BF16)
        x1, h2, route = _out_projection(o_a, o_cmp, o_slc, o_win, o_c, x, mix_gain[l].reshape(1, D),
                                        w_out[l].astype(BF16), g1, ln1_g[l].reshape(1, D), ln1_b[l].reshape(1, D),
                                        sc2, sh2, jnp.concatenate([w_r_hi, w_r_lo], axis=1), b_r, B, S)
        x = _moe(h2, x1, route, g2, ln2_g[l].reshape(1, D), ln2_b[l].reshape(1, D), w_gate, w_up, w_down, l, B, S)
    return x
```

```python
import functools

import numpy as np
import jax
import jax.numpy as jnp
from jax import lax
from jax.experimental import pallas as pl
from jax.experimental.pallas import tpu as pltpu

F32 = jnp.float32
BF16 = jnp.bfloat16
HIGHEST = lax.Precision.HIGHEST

D_MODEL = 2048
DEPTH = 2
HEAD_DIM = 64
HA = 8
HB = 8
HC = 16
KV_GROUPS = 2
REP = 4
N_STACK = KV_GROUPS * REP
SCALE = HEAD_DIM ** -0.5
WINDOW_A = 128
WINDOW_B = 512
CMP_BLOCK = 32
CMP_STRIDE = 16
CMP_HIDDEN = 128
SEL_BLOCK = 64
N_SELECT = 16
FORCE_SCORE = 1e4
N_GROUPS = 4
EXPERTS_PER_GROUP = 8
N_EXPERTS = 32
D_EXPERT = 512
ALPHA = (2.0 * DEPTH) ** 0.25
LANES = 128
VMEM_LIMIT = 48 * 1024 * 1024

SEG_SIZES = dict(qa=512, qb=512, ka=128, va=128, kbc=128, vbc=128, kbs=128, vbs=128,
                 kbw=128, vbw=128, gb=128, qc=1024, kc=1024, vc=1024)
SEG_ORDER = ("qa", "qb", "ka", "va", "kbc", "vbc", "kbs", "vbs", "kbw", "vbw", "gb", "qc", "kc", "vc")
SEG_OFF = {}
_o = 0
for _n in SEG_ORDER:
    SEG_OFF[_n] = _o
    _o += SEG_SIZES[_n]
PROJ_TN = 768
D_PROJ = -(-_o // PROJ_TN) * PROJ_TN
REF_SEGS = (("qa", 512), ("ka", 128), ("va", 128), ("qb", 512), ("kbc", 128), ("vbc", 128), ("kbs", 128),
            ("vbs", 128), ("kbw", 128), ("vbw", 128), ("gb", 24), ("qc", 1024), ("kc", 1024), ("vc", 1024))

SLOPES = [2.0 ** (-8.0 * (i + 1) / (HA + HB)) for i in range(HA + HB)]
STICK_EXIT = 105.0

ATT_TQ = 256
ATT_TK = 128
CMP_TQ = 128
CMP_PREFIXES = 4
STICK_T = 256
STICK_PAIRS = 2
STICK_FIRST = 2
M_INIT = -1e30


def _cparams(sem):
    return pltpu.CompilerParams(dimension_semantics=sem, vmem_limit_bytes=VMEM_LIMIT)


def _dot_nt(a, b):
    return lax.dot_general(a, b, (((1,), (1,)), ((), ())), preferred_element_type=F32)


def _dot_tn(a, b):
    return lax.dot_general(a, b, (((0,), (0,)), ((), ())), preferred_element_type=F32)


def _layer_norm(x, eps=1e-5):
    mu = jnp.mean(x, axis=-1, keepdims=True)
    xc = x - mu
    var = jnp.mean(xc * xc, axis=-1, keepdims=True)
    return xc * lax.rsqrt(var + eps)


def _ada_kernel(c_ref, w_ref, b_ref, o_ref):
    c = c_ref[...]
    cs = c * jax.nn.sigmoid(c)
    o_ref[0] = jnp.dot(cs, w_ref[0], preferred_element_type=F32, precision=HIGHEST) + b_ref[0]


def _ada_mod(c, w_ada, b_ada):
    B = c.shape[0]
    L, D, N6 = w_ada.shape
    cp = jnp.zeros((8, D), F32).at[:B].set(c)
    tn = 1024
    out = pl.pallas_call(
        _ada_kernel,
        grid=(L, N6 // tn),
        in_specs=[pl.BlockSpec((8, D), lambda l, j: (0, 0)),
                  pl.BlockSpec((1, D, tn), lambda l, j: (l, 0, j)),
                  pl.BlockSpec((1, 1, tn), lambda l, j: (l, 0, j))],
        out_specs=pl.BlockSpec((1, 8, tn), lambda l, j: (l, 0, j)),
        out_shape=jax.ShapeDtypeStruct((L, 8, N6), F32),
        compiler_params=_cparams(("arbitrary", "arbitrary")),
        name="ada_mod",
    )(cp, w_ada, b_ada.reshape(L, 1, N6))
    return out[:, :B]


def _inproj_kernel(x_ref, sc_ref, sh_ref, w_ref, o_ref, h_ref):
    @pl.when(pl.program_id(1) == 0)
    def _():
        h = _layer_norm(x_ref[...]) * (1.0 + sc_ref[0]) + sh_ref[0]
        h_ref[...] = h.astype(BF16)

    o_ref[...] = jnp.dot(h_ref[...], w_ref[...], preferred_element_type=F32).astype(BF16)


def _in_projection(x2d, sc, sh, w_packed, S):
    N, D = x2d.shape
    tm = 1024 if S % 1024 == 0 else S
    return pl.pallas_call(
        _inproj_kernel,
        grid=(N // tm, D_PROJ // PROJ_TN),
        in_specs=[pl.BlockSpec((tm, D), lambda i, j: (i, 0)),
                  pl.BlockSpec((1, 1, D), lambda i, j: (i * tm // S, 0, 0)),
                  pl.BlockSpec((1, 1, D), lambda i, j: (i * tm // S, 0, 0)),
                  pl.BlockSpec((D, PROJ_TN), lambda i, j: (0, j))],
        out_specs=pl.BlockSpec((tm, PROJ_TN), lambda i, j: (i, j)),
        out_shape=jax.ShapeDtypeStruct((N, D_PROJ), BF16),
        scratch_shapes=[pltpu.VMEM((tm, D), BF16)],
        compiler_params=_cparams(("arbitrary", "arbitrary")),
        name="ln_in_proj",
    )(x2d, sc, sh, w_packed)


def _pack_w_in(w_in_l):
    cols = {}
    off = 0
    for name, width in REF_SEGS:
        cols[name] = w_in_l[:, off:off + width]
        off += width
    parts = []
    for name in SEG_ORDER:
        wseg = cols[name]
        pad = SEG_SIZES[name] - wseg.shape[1]
        if pad:
            wseg = jnp.pad(wseg, ((0, 0), (0, pad)))
        parts.append(wseg)
    total = sum(SEG_SIZES.values())
    parts.append(jnp.zeros((w_in_l.shape[0], D_PROJ - total), w_in_l.dtype))
    return jnp.concatenate(parts, axis=1).astype(BF16)


def _compress_kernel(xa_ref, xb_ref, pe_ref, w1_ref, w2_ref, o_ref):
    for kv in range(2):
        xa = (xa_ref[kv, 0].astype(F32) + pe_ref[kv, 0:1, :]).astype(BF16)
        xb = (xb_ref[kv, 0].astype(F32) + pe_ref[kv, 1:2, :]).astype(BF16)
        hid = (jnp.dot(xa, w1_ref[kv, 0], preferred_element_type=F32)
               + jnp.dot(xb, w1_ref[kv, 1], preferred_element_type=F32))
        act = jax.nn.gelu(hid)
        o_ref[kv, 0] = jnp.dot(act.astype(BF16), w2_ref[kv], preferred_element_type=F32).astype(BF16)


def _compress(proj, pe_kv, w1_kv, w2_kv, B, S):
    nchunk = S // CMP_STRIDE
    ncp = nchunk
    width = CMP_STRIDE * KV_GROUPS * HEAD_DIM
    x16 = jnp.stack([proj[:, :, SEG_OFF[name]:SEG_OFF[name] + 128].reshape(B, nchunk, width)
                     for name in ("kbc", "vbc")])
    xb = jnp.concatenate([x16[:, :, 1:], jnp.zeros_like(x16[:, :, :1])], axis=2)
    eye = jnp.eye(KV_GROUPS, dtype=F32)
    w1 = jnp.einsum("kpldc,gh->kplgdhc", w1_kv.reshape(2, 2, CMP_STRIDE, HEAD_DIM, CMP_HIDDEN), eye)
    w1 = w1.reshape(2, 2, width, KV_GROUPS * CMP_HIDDEN).astype(BF16)
    w2 = jnp.einsum("kcd,gh->kgchd", w2_kv, eye).reshape(2, KV_GROUPS * CMP_HIDDEN, KV_GROUPS * HEAD_DIM).astype(BF16)
    pe = jnp.broadcast_to(pe_kv.reshape(2, 2, CMP_STRIDE, 1, HEAD_DIM),
                          (2, 2, CMP_STRIDE, KV_GROUPS, HEAD_DIM)).reshape(2, 2, width)
    blk = (2, 1, ncp, width)
    return pl.pallas_call(
        _compress_kernel,
        grid=(B,),
        in_specs=[pl.BlockSpec(blk, lambda b: (0, b, 0, 0)),
                  pl.BlockSpec(blk, lambda b: (0, b, 0, 0)),
                  pl.BlockSpec((2, 2, width), lambda b: (0, 0, 0)),
                  pl.BlockSpec((2, 2, width, KV_GROUPS * CMP_HIDDEN), lambda b: (0, 0, 0, 0)),
                  pl.BlockSpec((2, KV_GROUPS * CMP_HIDDEN, KV_GROUPS * HEAD_DIM), lambda b: (0, 0, 0))],
        out_specs=pl.BlockSpec((2, 1, ncp, KV_GROUPS * HEAD_DIM), lambda b: (0, b, 0, 0)),
        out_shape=jax.ShapeDtypeStruct((2, B, ncp, KV_GROUPS * HEAD_DIM), BF16),
        compiler_params=_cparams(("arbitrary",)),
        name="nsa_compress",
    )(x16, xb, pe, w1, w2)


def _stack_queries(q, tq):
    parts = []
    for h in range(N_STACK):
        q_h = q[:, h * HEAD_DIM:(h + 1) * HEAD_DIM]
        z = jnp.zeros_like(q_h)
        parts.append(jnp.concatenate([q_h, z] if h // REP == 0 else [z, q_h], axis=1))
    return jnp.concatenate(parts, axis=0) * SCALE


def _query_key_offsets(tk, tq):
    key = lax.broadcasted_iota(jnp.int32, (tk, N_STACK * tq), 0)
    qry = lax.broadcasted_iota(jnp.int32, (tk, N_STACK * tq), 1) & (tq - 1)
    return qry - key


def _flash_step(s, m, l, acc, v_t):
    m_new = jnp.maximum(m, jnp.max(s, axis=0, keepdims=True))
    a = jnp.exp(m - m_new)
    p = jnp.exp(s - m_new)
    l_new = a * l + jnp.sum(p, axis=0, keepdims=True)
    return m_new, l_new, a * acc + _dot_tn(v_t, p.astype(BF16))


def _store_heads(acc_n, gates_t, gate_col, o_ref, tq):
    for pair in range(N_STACK // 2):
        g = (2 * pair) // REP
        blocks = []
        for h in (2 * pair, 2 * pair + 1):
            blk = acc_n[g * HEAD_DIM:(g + 1) * HEAD_DIM, h * tq:(h + 1) * tq]
            if gates_t is not None:
                c = h * 3 + gate_col
                blk = blk * gates_t[c:c + 1, :]
            blocks.append(blk)
        o_ref[0, :, pair * LANES:(pair + 1) * LANES] = jnp.concatenate(blocks, axis=0).T


def _head_rows(values, tq):
    return jnp.repeat(jnp.asarray(values, F32), tq).reshape(1, N_STACK * tq)


def _band_kernel(*refs, window, use_sink, gate_col, tq, tk):
    if use_sink:
        slope_ref, sink_ref, q_ref, k_ref, v_ref, o_ref = refs
        g_ref = None
    else:
        slope_ref, q_ref, k_ref, v_ref, g_ref, o_ref = refs
    i = pl.program_id(1)
    n = N_STACK * tq
    per_q = tq // tk
    q_stack = _stack_queries(q_ref[0], tq)
    rel = _query_key_offsets(tk, tq)
    slope = slope_ref[...]
    bias0 = slope * rel.astype(F32)
    if use_sink:
        m, l = sink_ref[...], jnp.ones((1, n), F32)
    else:
        m, l = jnp.full((1, n), M_INIT, F32), jnp.zeros((1, n), F32)
    acc = jnp.zeros((LANES, n), F32)
    last = (i + 1) * per_q - 1
    for j in range(per_q + window // tk):
        c = j * tk - (tq - tk)
        start = pl.multiple_of(jnp.maximum(last - j, 0) * tk, tk)
        k_t = k_ref[0, pl.ds(start, tk), :]
        v_t = v_ref[0, pl.ds(start, tk), :]
        s = _dot_nt(k_t, q_stack) - (bias0 + slope * float(c))
        keep = None
        if c < tk - 1:
            keep = rel >= -c
        if c + tq - 1 >= window:
            far = rel < window - c
            keep = far if keep is None else keep & far
        if keep is not None:
            s = jnp.where(keep, s, -jnp.inf)
        if j >= per_q:
            s = jnp.where(last >= j, s, -jnp.inf)
        m, l, acc = _flash_step(s, m, l, acc, v_t)
    gates_t = None if g_ref is None else jax.nn.sigmoid(g_ref[0].astype(F32)).T
    _store_heads(acc * (1.0 / l), gates_t, gate_col, o_ref, tq)


def _band_attention(proj, qname, kname, vname, window, slopes, sinks, gate_col, B, S):
    tq, tk = ATT_TQ, ATT_TK
    n = N_STACK * tq
    kern = functools.partial(_band_kernel, window=window, use_sink=sinks is not None, gate_col=gate_col, tq=tq, tk=tk)
    qb, kb, vb, gbk = SEG_OFF[qname] // 512, SEG_OFF[kname] // 128, SEG_OFF[vname] // 128, SEG_OFF["gb"] // 128
    row = pl.BlockSpec((1, n), lambda b, i: (0, 0))
    specs = [pl.BlockSpec((1, tq, 512), lambda b, i: (b, i, qb)),
             pl.BlockSpec((1, S, 128), lambda b, i: (b, 0, kb)),
             pl.BlockSpec((1, S, 128), lambda b, i: (b, 0, vb))]
    args = [proj, proj, proj]
    if sinks is not None:
        specs = [row, row] + specs
        args = [_head_rows(slopes, tq), _head_rows(sinks, tq)] + args
    else:
        specs = [row] + specs + [pl.BlockSpec((1, tq, 128), lambda b, i: (b, i, gbk))]
        args = [_head_rows(slopes, tq)] + args + [proj]
    return pl.pallas_call(
        kern,
        grid=(B, S // tq),
        in_specs=specs,
        out_specs=pl.BlockSpec((1, tq, 512), lambda b, i: (b, i, 0)),
        out_shape=jax.ShapeDtypeStruct((B, S, 512), F32),
        compiler_params=_cparams(("arbitrary", "arbitrary")),
        name="band_attn_w%d" % window,
    )(*args)


def _cmp_select_kernel(slope_ref, q_ref, kc_ref, vc_ref, g_ref, ovt_ref, o_ref, sel_ref, cnt_ref, imp_s, *, tq, n_pick):
    i = pl.program_id(1)
    ncp = kc_ref.shape[2]
    n = N_STACK * tq
    q_stack = _stack_queries(q_ref[0], tq)
    t_row = i * tq + (lax.broadcasted_iota(jnp.int32, (1, n), 1) & (tq - 1))
    last_visible = (t_row - (CMP_BLOCK - 1)) >> (CMP_STRIDE.bit_length() - 1)
    gates_t = jax.nn.sigmoid(g_ref[0].astype(F32)).T

    def attend(rows):
        n_idx = lax.broadcasted_iota(jnp.int32, (rows, n), 0)
        s = _dot_nt(kc_ref[0, 0, :rows, :], q_stack) + slope_ref[:rows, :]
        s = jnp.where(n_idx <= last_visible, s, -jnp.inf)
        m = jnp.max(s, axis=0, keepdims=True)
        m = jnp.where(m == -jnp.inf, 0.0, m)
        e = jnp.exp(s - m)
        den = jnp.sum(e, axis=0, keepdims=True)
        p = e * (1.0 / jnp.maximum(den, 1e-30))
        _store_heads(_dot_tn(vc_ref[0, 0, :rows, :], p.astype(BF16)), gates_t, 0, o_ref, tq)
        psum = jnp.concatenate(
            [sum(p[:, (g * REP + r) * tq:(g * REP + r + 1) * tq] for r in range(REP)) for g in range(KV_GROUPS)],
            axis=1)
        p_hi = psum.astype(BF16)
        p_lo = (psum - p_hi.astype(F32)).astype(BF16)
        imp_s[...] = (jnp.dot(ovt_ref[:, :rows], p_hi, preferred_element_type=F32)
                      + jnp.dot(ovt_ref[:, :rows], p_lo, preferred_element_type=F32))

    needed = (i + 1) * tq // CMP_STRIDE
    prefixes = sorted({ncp * (v + 1) // CMP_PREFIXES for v in range(CMP_PREFIXES)})
    for idx, rows in enumerate(prefixes):
        below = prefixes[idx - 1] if idx else 0
        pl.when((needed > below) & (needed <= rows))(functools.partial(attend, rows))
    imp = imp_s[...]
    w = KV_GROUPS * tq
    j_idx = lax.broadcasted_iota(jnp.int32, (LANES, w), 0)
    t_sel = i * tq + (lax.broadcasted_iota(jnp.int32, (1, w), 1) & (tq - 1))
    cur = t_sel // SEL_BLOCK
    forced = (j_idx == 0) | (j_idx == cur) | (j_idx == cur - 1)
    valid = j_idx * SEL_BLOCK <= t_sel
    imp = jnp.where(valid, jnp.where(forced, FORCE_SCORE, imp), -1.0)

    def pick(_, carry):
        imp, sel = carry
        mx = jnp.max(imp, axis=0, keepdims=True)
        first = jnp.min(jnp.where(imp == mx, j_idx, LANES), axis=0, keepdims=True)
        hit = j_idx == first
        return jnp.where(hit, -2.0, imp), jnp.where(hit, 1.0, sel)

    _, sel = lax.fori_loop(0, n_pick, pick, (imp, jnp.zeros((LANES, w), F32)))
    sel_b = sel.astype(BF16)
    for g in range(KV_GROUPS):
        sel_ref[0, g] = sel_b[:, g * tq:(g + 1) * tq]
    cnt = _dot_nt(jnp.ones((8, w), BF16), sel_b)
    cnt_ref[0, 0] = cnt[0:1, :]


def _overlap_matrix_t(ncp):
    n = np.arange(ncp)[None, :]
    j = np.arange(LANES)[:, None]
    start = n * CMP_STRIDE
    end = start + CMP_BLOCK - 1
    return ((end >= j * SEL_BLOCK) & (start < j * SEL_BLOCK + SEL_BLOCK)).astype(np.float32)


def _cmp_select(proj, kvc, B, S):
    tq = CMP_TQ
    nq = S // tq
    ncp = kvc.shape[2]
    n = N_STACK * tq
    n_pick = min(N_SELECT, S // SEL_BLOCK)
    qb, gbk = SEG_OFF["qb"] // 512, SEG_OFF["gb"] // 128
    ovt = jnp.asarray(_overlap_matrix_t(ncp), BF16)
    block_end = jnp.arange(ncp, dtype=F32).reshape(ncp, 1) * CMP_STRIDE + (CMP_BLOCK - 1)
    key_bias = block_end * _head_rows(SLOPES[HA:], tq)
    kern = functools.partial(_cmp_select_kernel, tq=tq, n_pick=n_pick)
    kv_blk = (1, 1, ncp, LANES)
    return pl.pallas_call(
        kern,
        grid=(B, nq),
        in_specs=[pl.BlockSpec((ncp, n), lambda b, i: (0, 0)),
                  pl.BlockSpec((1, tq, 512), lambda b, i: (b, i, qb)),
                  pl.BlockSpec(kv_blk, lambda b, i: (0, b, 0, 0)),
                  pl.BlockSpec(kv_blk, lambda b, i: (1, b, 0, 0)),
                  pl.BlockSpec((1, tq, 128), lambda b, i: (b, i, gbk)),
                  pl.BlockSpec((LANES, ncp), lambda b, i: (0, 0))],
        out_specs=[pl.BlockSpec((1, tq, 512), lambda b, i: (b, i, 0)),
                   pl.BlockSpec((1, KV_GROUPS, LANES, tq), lambda b, i: (b, 0, 0, i)),
                   pl.BlockSpec((1, 1, 1, LANES), lambda b, i: (b, i, 0, 0))],
        out_shape=[jax.ShapeDtypeStruct((B, S, 512), F32),
                   jax.ShapeDtypeStruct((B, KV_GROUPS, LANES, S), BF16),
                   jax.ShapeDtypeStruct((B, nq, 1, LANES), F32)],
        scratch_shapes=[pltpu.VMEM((LANES, KV_GROUPS * tq), F32)],
        compiler_params=_cparams(("arbitrary", "arbitrary")),
        name="nsa_cmp_select",
    )(key_bias, proj, kvc, kvc, proj, ovt)


def _sel_kernel(flag_ref, slope_ref, q_ref, k_ref, v_ref, sel_ref, g_ref, o_ref, m_s, l_s, acc_s, *, tq, tk, nq):
    b = pl.program_id(0)
    i = pl.program_id(1)
    nk = nq * (tq // tk)
    per_q = tq // tk
    q_stack = _stack_queries(q_ref[0], tq)
    rel = _query_key_offsets(tk, tq)
    slope = slope_ref[...]
    bias0 = slope * rel.astype(F32)
    per_tile = tk // SEL_BLOCK
    key_blk = lax.broadcasted_iota(jnp.int32, (tk, LANES), 0) // SEL_BLOCK
    blk_id = lax.broadcasted_iota(jnp.int32, (tk, LANES), 1)
    m_s[...] = jnp.full(m_s.shape, M_INIT, F32)
    l_s[...] = jnp.zeros(l_s.shape, F32)
    acc_s[...] = jnp.zeros(acc_s.shape, F32)

    def tile(kt, diagonal):
        start = pl.multiple_of(kt * tk, tk)
        k_t = k_ref[0, pl.ds(start, tk), :]
        v_t = v_ref[0, pl.ds(start, tk), :]
        expand = jnp.where(blk_id == kt * per_tile + key_blk, 1.0, 0.0).astype(BF16)
        picked = [jnp.dot(expand, sel_ref[0, g], preferred_element_type=F32) for g in range(KV_GROUPS)]
        keep = jnp.concatenate([picked[h // REP] for h in range(N_STACK)], axis=1) > 0.5
        shift = i * tq - kt * tk
        if diagonal:
            keep = keep & (rel >= -shift)
        s = _dot_nt(k_t, q_stack) - (bias0 + slope * shift.astype(F32))
        s = jnp.where(keep, s, -jnp.inf)
        m, l, acc = _flash_step(s, m_s[...], l_s[...], acc_s[...], v_t)
        m_s[...] = m
        l_s[...] = l
        acc_s[...] = acc

    for d in range(per_q):
        tile((i + 1) * per_q - 1 - d, True)

    def body(j, _):
        kt = i * per_q - 1 - j

        @pl.when(flag_ref[(b * nq + i) * nk + kt] > 0)
        def _():
            tile(kt, False)

        return 0

    lax.fori_loop(0, i * per_q, body, 0)
    gates_t = jax.nn.sigmoid(g_ref[0].astype(F32)).T
    _store_heads(acc_s[...] * (1.0 / l_s[...]), gates_t, 1, o_ref, tq)


def _selected_attention(proj, sel, flags, B, S):
    tq, tk = ATT_TQ, ATT_TK
    nq = S // tq
    n = N_STACK * tq
    qb, kb, vb, gbk = SEG_OFF["qb"] // 512, SEG_OFF["kbs"] // 128, SEG_OFF["vbs"] // 128, SEG_OFF["gb"] // 128
    kern = functools.partial(_sel_kernel, tq=tq, tk=tk, nq=nq)
    grid_spec = pltpu.PrefetchScalarGridSpec(
        num_scalar_prefetch=1,
        grid=(B, nq),
        in_specs=[pl.BlockSpec((1, n), lambda b, i, f: (0, 0)),
                  pl.BlockSpec((1, tq, 512), lambda b, i, f: (b, i, qb)),
                  pl.BlockSpec((1, S, 128), lambda b, i, f: (b, 0, kb)),
                  pl.BlockSpec((1, S, 128), lambda b, i, f: (b, 0, vb)),
                  pl.BlockSpec((1, KV_GROUPS, LANES, tq), lambda b, i, f: (b, 0, 0, i)),
                  pl.BlockSpec((1, tq, 128), lambda b, i, f: (b, i, gbk))],
        out_specs=pl.BlockSpec((1, tq, 512), lambda b, i, f: (b, i, 0)),
        scratch_shapes=[pltpu.VMEM((1, n), F32), pltpu.VMEM((1, n), F32), pltpu.VMEM((LANES, n), F32)],
    )
    return pl.pallas_call(
        kern,
        grid_spec=grid_spec,
        out_shape=jax.ShapeDtypeStruct((B, S, 512), F32),
        compiler_params=_cparams(("arbitrary", "arbitrary")),
        name="nsa_selected",
    )(flags, _head_rows(SLOPES[HA:], tq), proj, proj, proj, sel, proj)


def _selection_flags(cnt, B, S):
    nq, nk = S // ATT_TQ, S // ATT_TK
    per_tile = ATT_TK // SEL_BLOCK
    c = cnt[:, :, 0, :nk * per_tile].reshape(B, nq, ATT_TQ // CMP_TQ, nk, per_tile).sum((2, 4))
    return (c > 0).astype(jnp.int32).reshape(-1)


def _stick_kernel(*refs, t, pairs):
    o_ref = refs[3 * pairs]
    i = pl.program_id(2)
    started = [_stick_first_tiles(i, refs[p], refs[pairs + p], refs[2 * pairs + p], t) for p in range(pairs)]
    for p, (carry, acc, step) in enumerate(started):
        def cond(state):
            kt, go, _, _ = state
            return (kt >= 0) & (go > 0)

        def body(state, step=step):
            kt, _, carry, acc = state
            carry, acc = step(kt, carry, acc)
            return kt - 1, (jnp.max(carry) > -STICK_EXIT).astype(jnp.int32), carry, acc

        go0 = (jnp.max(carry) > -STICK_EXIT).astype(jnp.int32)
        _, _, _, acc = lax.while_loop(cond, body, (i - STICK_FIRST, go0, carry, acc))
        o_ref[0, :, p * LANES:(p + 1) * LANES] = jnp.concatenate([acc[:HEAD_DIM, :t], acc[HEAD_DIM:, t:]], axis=0).T


def _stick_first_tiles(i, q_ref, k_ref, v_ref, t):
    n = 2 * t
    lane = lax.broadcasted_iota(jnp.int32, (t, LANES), 1)
    q = q_ref[0]
    zero = jnp.zeros_like(q)
    q_stack = jnp.concatenate([jnp.where(lane < HEAD_DIM, q, zero), jnp.where(lane >= HEAD_DIM, q, zero)], axis=0) * SCALE
    key = lax.broadcasted_iota(jnp.int32, (t, n), 0)
    qry = lax.broadcasted_iota(jnp.int32, (t, n), 1) & (t - 1)
    causal = key < qry
    r_i = lax.broadcasted_iota(jnp.int32, (t, t), 0)
    c_i = lax.broadcasted_iota(jnp.int32, (t, t), 1)
    suffix = jnp.where(c_i >= r_i, 1.0, 0.0).astype(BF16)
    suffix2 = jnp.concatenate([suffix, suffix], axis=1)

    def scores(kt, masked):
        start = pl.multiple_of(kt * t, t)
        z = _dot_nt(k_ref[0, pl.ds(start, t), :], q_stack)
        log_1mb = -(jnp.maximum(z, 0.0) + jnp.log(1.0 + jnp.exp(-jnp.abs(z))))
        if masked:
            log_1mb = jnp.where(causal, log_1mb, 0.0)
        hi = log_1mb.astype(BF16)
        lo = (log_1mb - hi.astype(F32)).astype(BF16)
        inc = jnp.dot(suffix2, jnp.concatenate([hi, lo], axis=0),
                      preferred_element_type=F32)
        return z, inc

    def weights(z, inc, carry):
        return jnp.exp(z + inc + carry)

    def values(kt, a):
        start = pl.multiple_of(kt * t, t)
        return _dot_tn(v_ref[0, pl.ds(start, t), :], a.astype(BF16))

    z0, inc0 = scores(i, True)
    before = [(d, jnp.maximum(i - d, 0)) for d in range(1, STICK_FIRST)]
    ahead = [scores(kt, False) for _, kt in before]
    carry = inc0[0:1, :]
    acc = values(i, jnp.where(causal, weights(z0, inc0, 0.0), 0.0))
    for (d, kt), (z_d, inc_d) in zip(before, ahead):
        acc = acc + values(kt, jnp.where(i >= d, weights(z_d, inc_d, carry), 0.0))
        carry = carry + jnp.where(i >= d, inc_d[0:1, :], 0.0)

    def step(kt, carry, acc):
        z, inc = scores(kt, False)
        return carry + inc[0:1, :], acc + values(kt, weights(z, inc, carry))

    return carry, acc, step


def _stick_attention(proj, B, S):
    t = min(STICK_T, S)
    pairs = STICK_PAIRS
    qb, kb, vb = SEG_OFF["qc"] // 128, SEG_OFF["kc"] // 128, SEG_OFF["vc"] // 128
    col = lambda base, p: (lambda b, g, i: (b, i, base + g * pairs + p))
    whole = lambda base, p: (lambda b, g, i: (b, 0, base + g * pairs + p))
    specs = ([pl.BlockSpec((1, t, 128), col(qb, p)) for p in range(pairs)]
             + [pl.BlockSpec((1, S, 128), whole(kb, p)) for p in range(pairs)]
             + [pl.BlockSpec((1, S, 128), whole(vb, p)) for p in range(pairs)])
    return pl.pallas_call(
        functools.partial(_stick_kernel, t=t, pairs=pairs),
        grid=(B, HC // 2 // pairs, S // t),
        in_specs=specs,
        out_specs=pl.BlockSpec((1, t, pairs * LANES), lambda b, g, i: (b, i, g)),
        out_shape=jax.ShapeDtypeStruct((B, S, HC * HEAD_DIM), F32),
        compiler_params=_cparams(("arbitrary", "arbitrary", "arbitrary")),
        name="stick_breaking",
    )(*([proj] * (3 * pairs)))


def _group_rms(o, eps=1e-6):
    return o * lax.rsqrt(jnp.mean(o * o, axis=-1, keepdims=True) + eps)


def _outproj_kernel(oa_ref, ob1_ref, ob2_ref, ob3_ref, oc_ref, x_ref, gain_ref, w_ref, g1_ref, lg_ref, lb_ref,
                    sc2_ref, sh2_ref, wr_ref, br_ref, x1_ref, h2_ref, route_ref):
    o_b = ob1_ref[0] + ob2_ref[0] + ob3_ref[0]
    merged = jnp.concatenate([_group_rms(oa_ref[0]), _group_rms(o_b), _group_rms(oc_ref[0])], axis=-1)
    merged = (merged * gain_ref[...]).astype(BF16)
    y = jnp.dot(merged, w_ref[...], preferred_element_type=F32)
    x1 = _layer_norm(ALPHA * x_ref[0] + (1.0 + g1_ref[0]) * y) * lg_ref[...] + lb_ref[...]
    x1_ref[0] = x1
    h2 = _layer_norm(x1) * (1.0 + sc2_ref[0]) + sh2_ref[0]
    h2_ref[0] = h2
    h_hi = h2.astype(BF16)
    h_lo = (h2 - h_hi.astype(F32)).astype(BF16)
    part = jnp.dot(h_hi, wr_ref[...], preferred_element_type=F32)
    logits = (part[:, :LANES] + part[:, LANES:]
              + jnp.dot(h_lo, wr_ref[:, :LANES], preferred_element_type=F32) + br_ref[...])
    tm = logits.shape[0]
    lane = lax.broadcasted_iota(jnp.int32, (tm, LANES), 1)
    lg = jnp.where(lane < N_GROUPS, logits, -jnp.inf)
    mg = jnp.max(lg, axis=-1, keepdims=True)
    p_group = 1.0 / jnp.sum(jnp.exp(lg - mg), axis=-1, keepdims=True)
    g_sel = jnp.min(jnp.where(lg == mg, lane, LANES), axis=-1, keepdims=True)
    e_idx = lane - N_GROUPS
    in_grp = (e_idx >= g_sel * EXPERTS_PER_GROUP) & (e_idx < (g_sel + 1) * EXPERTS_PER_GROUP)
    le = jnp.where(in_grp, logits, -jnp.inf)
    m1 = jnp.max(le, axis=-1, keepdims=True)
    i1 = jnp.min(jnp.where(le == m1, e_idx, LANES), axis=-1, keepdims=True)
    le2 = jnp.where(e_idx == i1, -jnp.inf, le)
    m2 = jnp.max(le2, axis=-1, keepdims=True)
    i2 = jnp.min(jnp.where(le2 == m2, e_idx, LANES), axis=-1, keepdims=True)
    den = jnp.sum(jnp.exp(le - m1), axis=-1, keepdims=True)
    p1 = 1.0 / den
    p2 = jnp.exp(m2 - m1) / den
    gate1 = p_group * p1 / (p1 + p2)
    gate2 = p_group * p2 / (p1 + p2)
    route = jnp.where(lane == 0, i1.astype(F32),
                      jnp.where(lane == 1, i2.astype(F32),
                                jnp.where(lane == 2, gate1, jnp.where(lane == 3, gate2, 0.0))))
    route_ref[0] = route


def _out_projection(o_a, o_b1, o_b2, o_b3, o_c, x, gain, w_out, g1, ln_g, ln_b, sc2, sh2, w_r, b_r, B, S):
    D = D_MODEL
    tm = 256
    row = lambda b, i: (b, i, 0)
    vec = lambda b, i: (0, 0)
    bvec = lambda b, i: (b, 0, 0)
    return pl.pallas_call(
        _outproj_kernel,
        grid=(B, S // tm),
        in_specs=[pl.BlockSpec((1, tm, 512), row), pl.BlockSpec((1, tm, 512), row),
                  pl.BlockSpec((1, tm, 512), row), pl.BlockSpec((1, tm, 512), row),
                  pl.BlockSpec((1, tm, 1024), row), pl.BlockSpec((1, tm, D), row),
                  pl.BlockSpec((1, D), vec), pl.BlockSpec((D, D), vec),
                  pl.BlockSpec((1, 1, D), bvec), pl.BlockSpec((1, D), vec), pl.BlockSpec((1, D), vec),
                  pl.BlockSpec((1, 1, D), bvec), pl.BlockSpec((1, 1, D), bvec),
                  pl.BlockSpec((D, 2 * LANES), vec), pl.BlockSpec((1, LANES), vec)],
        out_specs=[pl.BlockSpec((1, tm, D), row), pl.BlockSpec((1, tm, D), row),
                   pl.BlockSpec((1, tm, LANES), row)],
        out_shape=[jax.ShapeDtypeStruct((B, S, D), F32), jax.ShapeDtypeStruct((B, S, D), F32),
                   jax.ShapeDtypeStruct((B, S, LANES), F32)],
        compiler_params=_cparams(("arbitrary", "arbitrary")),
        name="out_proj_norm_route",
    )(o_a, o_b1, o_b2, o_b3, o_c, x, gain, w_out, g1, ln_g, ln_b, sc2, sh2, w_r, b_r)


ROW_UNROLL = 8


def _row_copy(src_ref, src_row, dst_ref, dst_row, sem):
    return pltpu.make_async_copy(src_ref.at[pl.ds(src_row, 1)], dst_ref.at[pl.ds(dst_row, 1)], sem)


def _dispatch_kernel(dest_ref, pend_ref, h_ref, xs_ref, zero_s, sem, zsem, *, tm, bm):
    i = pl.program_id(0)

    @pl.when(i == 0)
    def _():
        zero_s[...] = jnp.zeros(zero_s.shape, F32)

        def zero_copy(e):
            start = pl.multiple_of(jnp.maximum(pend_ref[e] - bm, 0), bm)
            return pltpu.make_async_copy(zero_s, xs_ref.at[pl.ds(start, bm)], zsem)

        def has_rows(e):
            return pend_ref[e] > (pend_ref[e - 1] if e else 0)

        for e in range(N_EXPERTS):
            pl.when(has_rows(e))(lambda e=e: zero_copy(e).start())
        for e in range(N_EXPERTS):
            pl.when(has_rows(e))(lambda e=e: zero_copy(e).wait())

        def tail_copy(blk):
            return pltpu.make_async_copy(zero_s, xs_ref.at[pl.ds(pl.multiple_of(blk * bm, bm), bm)], zsem)

        def tail_start(blk, carry):
            tail_copy(blk).start()
            return carry

        def tail_wait(blk, carry):
            tail_copy(blk).wait()
            return carry

        first_free = pend_ref[N_EXPERTS - 1] // bm
        lax.fori_loop(first_free, xs_ref.shape[0] // bm, tail_start, 0)
        lax.fori_loop(first_free, xs_ref.shape[0] // bm, tail_wait, 0)

    def body(blk, _):
        for u in range(ROW_UNROLL):
            r = blk * ROW_UNROLL + u
            for k in range(2):
                _row_copy(h_ref, r, xs_ref, dest_ref[(i * tm + r) * 2 + k], sem).start()
        return 0

    lax.fori_loop(0, tm // ROW_UNROLL, body, 0)
    for k in range(2):
        pltpu.make_async_copy(h_ref, xs_ref.at[pl.ds(0, tm)], sem).wait()


def _dispatch(dest, pad_end, h2, rows, bm):
    N, D = h2.shape
    tm = MOE_TM
    grid_spec = pltpu.PrefetchScalarGridSpec(
        num_scalar_prefetch=2,
        grid=(N // tm,),
        in_specs=[pl.BlockSpec((tm, D), lambda i, d, p: (i, 0))],
        out_specs=pl.BlockSpec(memory_space=pl.ANY),
        scratch_shapes=[pltpu.VMEM((bm, D), F32), pltpu.SemaphoreType.DMA(()), pltpu.SemaphoreType.DMA(())],
    )
    return pl.pallas_call(
        functools.partial(_dispatch_kernel, tm=tm, bm=bm),
        grid_spec=grid_spec,
        out_shape=jax.ShapeDtypeStruct((rows, D), F32),
        compiler_params=_cparams(("arbitrary",)),
        name="moe_dispatch",
    )(dest, pad_end, h2)


def _expert_kernel(be_ref, nb_ref, slot_ref, next_ref, x_ref, wg_hbm, wu_hbm, wd_hbm, y_ref,
                   fg, fu, fd, wg_s, wu_s, wd_s, sem, *, layer):
    i = pl.program_id(0)

    def fetch(e, slot):
        return [pltpu.make_async_copy(wg_hbm.at[layer, e], fg.at[slot], sem.at[slot]),
                pltpu.make_async_copy(wu_hbm.at[layer, e], fu.at[slot], sem.at[slot]),
                pltpu.make_async_copy(wd_hbm.at[layer, e], fd.at[slot], sem.at[slot])]

    @pl.when(i < nb_ref[0])
    def _():
        e = be_ref[i]
        slot = slot_ref[i]

        @pl.when(i == 0)
        def _():
            for cp in fetch(e, slot):
                cp.start()

        @pl.when((i == 0) | (e != be_ref[jnp.maximum(i - 1, 0)]))
        def _():
            for cp in fetch(e, slot):
                cp.wait()
            wg_s[...] = fg[slot].astype(BF16)
            wu_s[...] = fu[slot].astype(BF16)
            wd_s[...] = fd[slot].astype(BF16)

            @pl.when(next_ref[i] >= 0)
            def _():
                for cp in fetch(next_ref[i], 1 - slot):
                    cp.start()

        x = x_ref[...].astype(BF16)
        hg = jnp.dot(x, wg_s[...], preferred_element_type=F32)
        hu = jnp.dot(x, wu_s[...], preferred_element_type=F32)
        hid = (hg * jax.nn.sigmoid(hg)) * hu
        y_ref[...] = jnp.dot(hid.astype(BF16), wd_s[...], preferred_element_type=F32)

    @pl.when(i >= nb_ref[0])
    def _():
        y_ref[...] = jnp.zeros(y_ref.shape, F32)


def _expert_mlp(blk_expert, n_used, blk_slot, blk_next, xs, w_gate, w_up, w_down, layer, bm):
    rows, D = xs.shape
    nblk = rows // bm
    blk = lambda i, be, nb, sl, nx: (jnp.minimum(i, nb[0] - 1), 0)
    hbm = pl.BlockSpec(memory_space=pl.ANY)
    grid_spec = pltpu.PrefetchScalarGridSpec(
        num_scalar_prefetch=4,
        grid=(nblk,),
        in_specs=[pl.BlockSpec((bm, D), blk), hbm, hbm, hbm],
        out_specs=pl.BlockSpec((bm, D), lambda i, be, nb, sl, nx: (i, 0)),
        scratch_shapes=[pltpu.VMEM((2, D, D_EXPERT), F32), pltpu.VMEM((2, D, D_EXPERT), F32),
                        pltpu.VMEM((2, D_EXPERT, D), F32),
                        pltpu.VMEM((D, D_EXPERT), BF16), pltpu.VMEM((D, D_EXPERT), BF16),
                        pltpu.VMEM((D_EXPERT, D), BF16), pltpu.SemaphoreType.DMA((2,))],
    )
    return pl.pallas_call(
        functools.partial(_expert_kernel, layer=layer),
        grid_spec=grid_spec,
        out_shape=jax.ShapeDtypeStruct((rows, D), F32),
        compiler_params=_cparams(("arbitrary",)),
        name="expert_mlp",
    )(blk_expert, n_used, blk_slot, blk_next, xs, w_gate, w_up, w_down)


def _combine_kernel(dest_ref, x_ref, route_ref, g2_ref, lg_ref, lb_ref, yb_ref, o_ref, buf, sem, *, tm):
    i = pl.program_id(0)

    def body(blk, _):
        for u in range(ROW_UNROLL):
            r = blk * ROW_UNROLL + u
            for k in range(2):
                _row_copy(yb_ref, dest_ref[(i * tm + r) * 2 + k], buf.at[k], r, sem).start()
        return 0

    lax.fori_loop(0, tm // ROW_UNROLL, body, 0)
    for k in range(2):
        pltpu.make_async_copy(yb_ref.at[pl.ds(0, tm)], buf.at[k], sem).wait()
    route = route_ref[...]
    y = buf[0] * route[:, 2:3] + buf[1] * route[:, 3:4]
    o_ref[...] = _layer_norm(ALPHA * x_ref[...] + (1.0 + g2_ref[0]) * y) * lg_ref[...] + lb_ref[...]


def _combine_post_norm(dest, x1, route, g2, ln_g, ln_b, yb, S):
    N, D = x1.shape
    tm = MOE_TM
    grid_spec = pltpu.PrefetchScalarGridSpec(
        num_scalar_prefetch=1,
        grid=(N // tm,),
        in_specs=[pl.BlockSpec((tm, D), lambda i, d: (i, 0)),
                  pl.BlockSpec((tm, LANES), lambda i, d: (i, 0)),
                  pl.BlockSpec((1, 1, D), lambda i, d: (i * tm // S, 0, 0)),
                  pl.BlockSpec((1, D), lambda i, d: (0, 0)), pl.BlockSpec((1, D), lambda i, d: (0, 0)),
                  pl.BlockSpec(memory_space=pl.ANY)],
        out_specs=pl.BlockSpec((tm, D), lambda i, d: (i, 0)),
        scratch_shapes=[pltpu.VMEM((2, tm, D), F32), pltpu.SemaphoreType.DMA(())],
    )
    return pl.pallas_call(
        functools.partial(_combine_kernel, tm=tm),
        grid_spec=grid_spec,
        out_shape=jax.ShapeDtypeStruct((N, D), F32),
        compiler_params=_cparams(("arbitrary",)),
        name="moe_combine_post_norm",
    )(dest, x1, route, g2, ln_g, ln_b, yb)


MOE_BM = 256
MOE_TM = 512


def _moe(h2, x1, route, g2, ln_g, ln_b, w_gate, w_up, w_down, layer, B, S):
    D = D_MODEL
    N = B * S
    A = 2 * N
    bm = MOE_BM
    r = route.reshape(N, LANES)
    e_flat = r[:, 0:2].astype(jnp.int32).reshape(A)
    onehot = (e_flat[:, None] == jnp.arange(N_EXPERTS)[None, :]).astype(jnp.int32)
    csum = jnp.cumsum(onehot, axis=0)
    counts = csum[-1]
    rank = jnp.sum((csum - onehot) * onehot, axis=1)
    padded = (counts + bm - 1) // bm * bm
    pad_end = jnp.cumsum(padded).astype(jnp.int32)
    pad_start = pad_end - padded
    dest = (pad_start[e_flat] + rank).astype(jnp.int32)
    nblk = -(-A // bm) + N_EXPERTS
    blk_start = jnp.arange(nblk) * bm
    blk_expert = jnp.minimum(jnp.sum(blk_start[:, None] >= pad_end[None, :], axis=1), N_EXPERTS - 1).astype(jnp.int32)
    n_used = (pad_end[-1] // bm).astype(jnp.int32).reshape(1)
    xs = _dispatch(dest, pad_end, h2.reshape(N, D), nblk * bm, bm)
    first_blk = jnp.concatenate([jnp.ones((1,), bool), blk_expert[1:] != blk_expert[:-1]])
    blk_slot = ((jnp.cumsum(first_blk) - 1) & 1).astype(jnp.int32)
    after = pad_end[blk_expert] // bm
    blk_next = jnp.where(after < n_used[0], blk_expert[jnp.minimum(after, nblk - 1)], -1).astype(jnp.int32)
    yb = _expert_mlp(blk_expert, n_used, blk_slot, blk_next, xs, w_gate, w_up, w_down, layer, bm)
    out = _combine_post_norm(dest, x1.reshape(N, D), r, g2, ln_g, ln_b, yb, S)
    return out.reshape(B, S, D)


def kernel(x, c, w_ada, b_ada, w_in, sinks, cmp_pe_k, cmp_w1_k, cmp_w2_k, cmp_pe_v, cmp_w1_v, cmp_w2_v, mix_gain, w_out, ln1_g, ln1_b, w_rg, b_rg, w_re, b_re, w_gate, w_up, w_down, ln2_g, ln2_b):
    B, S, D = x.shape
    L = w_ada.shape[0]
    mod = _ada_mod(c, w_ada, b_ada)
    for l in range(L):
        sh1, sc1, g1, sh2, sc2, g2 = [mod[l, :, k * D:(k + 1) * D].reshape(B, 1, D) for k in range(6)]
        proj = _in_projection(x.reshape(B * S, D), sc1, sh1, _pack_w_in(w_in[l]), S).reshape(B, S, D_PROJ)
        o_a = _band_attention(proj, "qa", "ka", "va", WINDOW_A, SLOPES[:HA], sinks[l], 0, B, S)
        kvc = _compress(proj, jnp.stack([cmp_pe_k[l], cmp_pe_v[l]]), jnp.stack([cmp_w1_k[l], cmp_w1_v[l]]),
                        jnp.stack([cmp_w2_k[l], cmp_w2_v[l]]), B, S)
        o_cmp, sel, cnt = _cmp_select(proj, kvc, B, S)
        o_slc = _selected_attention(proj, sel, _selection_flags(cnt, B, S), B, S)
        o_win = _band_attention(proj, "qb", "kbw", "vbw", WINDOW_B, SLOPES[HA:], None, 2, B, S)
        o_c = _stick_attention(proj, B, S)
        w_r = jnp.concatenate([w_rg[l], w_re[l].transpose(1, 0, 2).reshape(D, N_EXPERTS),
                               jnp.zeros((D, LANES - N_GROUPS - N_EXPERTS), F32)], axis=1)
        b_r = jnp.concatenate([b_rg[l], b_re[l].reshape(N_EXPERTS),
                               jnp.zeros((LANES - N_GROUPS - N_EXPERTS,), F32)]).reshape(1, LANES)
        w_r_hi = w_r.astype(BF16)
        w_r_lo = (w_r - w_r_hi.astype(F32)).astype(BF16)
        x1, h2, route = _out_projection(o_a, o_cmp, o_slc, o_win, o_c, x, mix_gain[l].reshape(1, D),
                                        w_out[l].astype(BF16), g1, ln1_g[l].reshape(1, D), ln1_b[l].reshape(1, D),
                                        sc2, sh2, jnp.concatenate([w_r_hi, w_r_lo], axis=1), b_r, B, S)
        x = _moe(h2, x1, route, g2, ln2_g[l].reshape(1, D), ln2_b[l].reshape(1, D), w_gate, w_up, w_down, l, B, S)
    return x
```

```python
import functools

import numpy as np
import jax
import jax.numpy as jnp
from jax import lax
from jax.experimental import pallas as pl
from jax.experimental.pallas import tpu as pltpu

F32 = jnp.float32
BF16 = jnp.bfloat16
HIGHEST = lax.Precision.HIGHEST

D_MODEL = 2048
DEPTH = 2
HEAD_DIM = 64
HA = 8
HB = 8
HC = 16
KV_GROUPS = 2
REP = 4
N_STACK = KV_GROUPS * REP
SCALE = HEAD_DIM ** -0.5
WINDOW_A = 128
WINDOW_B = 512
CMP_BLOCK = 32
CMP_STRIDE = 16
CMP_HIDDEN = 128
SEL_BLOCK = 64
N_SELECT = 16
FORCE_SCORE = 1e4
N_GROUPS = 4
EXPERTS_PER_GROUP = 8
N_EXPERTS = 32
D_EXPERT = 512
ALPHA = (2.0 * DEPTH) ** 0.25
LANES = 128
VMEM_LIMIT = 48 * 1024 * 1024

SEG_SIZES = dict(qa=512, qb=512, ka=128, va=128, kbc=128, vbc=128, kbs=128, vbs=128,
                 kbw=128, vbw=128, gb=128, qc=1024, kc=1024, vc=1024)
SEG_ORDER = ("qa", "qb", "ka", "va", "kbc", "vbc", "kbs", "vbs", "kbw", "vbw", "gb", "qc", "kc", "vc")
SEG_OFF = {}
_o = 0
for _n in SEG_ORDER:
    SEG_OFF[_n] = _o
    _o += SEG_SIZES[_n]
PROJ_TN = 768
D_PROJ = -(-_o // PROJ_TN) * PROJ_TN
REF_SEGS = (("qa", 512), ("ka", 128), ("va", 128), ("qb", 512), ("kbc", 128), ("vbc", 128), ("kbs", 128),
            ("vbs", 128), ("kbw", 128), ("vbw", 128), ("gb", 24), ("qc", 1024), ("kc", 1024), ("vc", 1024))

SLOPES = [2.0 ** (-8.0 * (i + 1) / (HA + HB)) for i in range(HA + HB)]
STICK_EXIT = 105.0

ATT_TQ = 256
ATT_TK = 128
CMP_TQ = 128
CMP_PREFIXES = 4
STICK_T = 256
STICK_PAIRS = 4
STICK_FIRST = 2
M_INIT = -1e30


def _cparams(sem):
    return pltpu.CompilerParams(dimension_semantics=sem, vmem_limit_bytes=VMEM_LIMIT)


def _dot_nt(a, b):
    return lax.dot_general(a, b, (((1,), (1,)), ((), ())), preferred_element_type=F32)


def _dot_tn(a, b):
    return lax.dot_general(a, b, (((0,), (0,)), ((), ())), preferred_element_type=F32)


def _layer_norm(x, eps=1e-5):
    mu = jnp.mean(x, axis=-1, keepdims=True)
    xc = x - mu
    var = jnp.mean(xc * xc, axis=-1, keepdims=True)
    return xc * lax.rsqrt(var + eps)


def _ada_kernel(c_ref, w_ref, b_ref, o_ref):
    c = c_ref[...]
    cs = c * jax.nn.sigmoid(c)
    o_ref[0] = jnp.dot(cs, w_ref[0], preferred_element_type=F32, precision=HIGHEST) + b_ref[0]


def _ada_mod(c, w_ada, b_ada):
    B = c.shape[0]
    L, D, N6 = w_ada.shape
    cp = jnp.zeros((8, D), F32).at[:B].set(c)
    tn = 1024
    out = pl.pallas_call(
        _ada_kernel,
        grid=(L, N6 // tn),
        in_specs=[pl.BlockSpec((8, D), lambda l, j: (0, 0)),
                  pl.BlockSpec((1, D, tn), lambda l, j: (l, 0, j)),
                  pl.BlockSpec((1, 1, tn), lambda l, j: (l, 0, j))],
        out_specs=pl.BlockSpec((1, 8, tn), lambda l, j: (l, 0, j)),
        out_shape=jax.ShapeDtypeStruct((L, 8, N6), F32),
        compiler_params=_cparams(("arbitrary", "arbitrary")),
        name="ada_mod",
    )(cp, w_ada, b_ada.reshape(L, 1, N6))
    return out[:, :B]


def _inproj_kernel(x_ref, sc_ref, sh_ref, w_ref, o_ref, h_ref):
    @pl.when(pl.program_id(1) == 0)
    def _():
        h = _layer_norm(x_ref[...]) * (1.0 + sc_ref[0]) + sh_ref[0]
        h_ref[...] = h.astype(BF16)

    o_ref[...] = jnp.dot(h_ref[...], w_ref[...], preferred_element_type=F32).astype(BF16)


def _in_projection(x2d, sc, sh, w_packed, S):
    N, D = x2d.shape
    tm = 1024 if S % 1024 == 0 else S
    return pl.pallas_call(
        _inproj_kernel,
        grid=(N // tm, D_PROJ // PROJ_TN),
        in_specs=[pl.BlockSpec((tm, D), lambda i, j: (i, 0)),
                  pl.BlockSpec((1, 1, D), lambda i, j: (i * tm // S, 0, 0)),
                  pl.BlockSpec((1, 1, D), lambda i, j: (i * tm // S, 0, 0)),
                  pl.BlockSpec((D, PROJ_TN), lambda i, j: (0, j))],
        out_specs=pl.BlockSpec((tm, PROJ_TN), lambda i, j: (i, j)),
        out_shape=jax.ShapeDtypeStruct((N, D_PROJ), BF16),
        scratch_shapes=[pltpu.VMEM((tm, D), BF16)],
        compiler_params=_cparams(("arbitrary", "arbitrary")),
        name="ln_in_proj",
    )(x2d, sc, sh, w_packed)


def _pack_w_in(w_in_l):
    cols = {}
    off = 0
    for name, width in REF_SEGS:
        cols[name] = w_in_l[:, off:off + width]
        off += width
    parts = []
    for name in SEG_ORDER:
        wseg = cols[name]
        pad = SEG_SIZES[name] - wseg.shape[1]
        if pad:
            wseg = jnp.pad(wseg, ((0, 0), (0, pad)))
        parts.append(wseg)
    total = sum(SEG_SIZES.values())
    parts.append(jnp.zeros((w_in_l.shape[0], D_PROJ - total), w_in_l.dtype))
    return jnp.concatenate(parts, axis=1).astype(BF16)


def _compress_kernel(xa_ref, xb_ref, pe_ref, w1_ref, w2_ref, o_ref):
    for kv in range(2):
        xa = (xa_ref[kv, 0].astype(F32) + pe_ref[kv, 0:1, :]).astype(BF16)
        xb = (xb_ref[kv, 0].astype(F32) + pe_ref[kv, 1:2, :]).astype(BF16)
        hid = (jnp.dot(xa, w1_ref[kv, 0], preferred_element_type=F32)
               + jnp.dot(xb, w1_ref[kv, 1], preferred_element_type=F32))
        act = jax.nn.gelu(hid)
        o_ref[kv, 0] = jnp.dot(act.astype(BF16), w2_ref[kv], preferred_element_type=F32).astype(BF16)


def _compress(proj, pe_kv, w1_kv, w2_kv, B, S):
    nchunk = S // CMP_STRIDE
    ncp = nchunk
    width = CMP_STRIDE * KV_GROUPS * HEAD_DIM
    x16 = jnp.stack([proj[:, :, SEG_OFF[name]:SEG_OFF[name] + 128].reshape(B, nchunk, width)
                     for name in ("kbc", "vbc")])
    xb = jnp.concatenate([x16[:, :, 1:], jnp.zeros_like(x16[:, :, :1])], axis=2)
    eye = jnp.eye(KV_GROUPS, dtype=F32)
    w1 = jnp.einsum("kpldc,gh->kplgdhc", w1_kv.reshape(2, 2, CMP_STRIDE, HEAD_DIM, CMP_HIDDEN), eye)
    w1 = w1.reshape(2, 2, width, KV_GROUPS * CMP_HIDDEN).astype(BF16)
    w2 = jnp.einsum("kcd,gh->kgchd", w2_kv, eye).reshape(2, KV_GROUPS * CMP_HIDDEN, KV_GROUPS * HEAD_DIM).astype(BF16)
    pe = jnp.broadcast_to(pe_kv.reshape(2, 2, CMP_STRIDE, 1, HEAD_DIM),
                          (2, 2, CMP_STRIDE, KV_GROUPS, HEAD_DIM)).reshape(2, 2, width)
    blk = (2, 1, ncp, width)
    return pl.pallas_call(
        _compress_kernel,
        grid=(B,),
        in_specs=[pl.BlockSpec(blk, lambda b: (0, b, 0, 0)),
                  pl.BlockSpec(blk, lambda b: (0, b, 0, 0)),
                  pl.BlockSpec((2, 2, width), lambda b: (0, 0, 0)),
                  pl.BlockSpec((2, 2, width, KV_GROUPS * CMP_HIDDEN), lambda b: (0, 0, 0, 0)),
                  pl.BlockSpec((2, KV_GROUPS * CMP_HIDDEN, KV_GROUPS * HEAD_DIM), lambda b: (0, 0, 0))],
        out_specs=pl.BlockSpec((2, 1, ncp, KV_GROUPS * HEAD_DIM), lambda b: (0, b, 0, 0)),
        out_shape=jax.ShapeDtypeStruct((2, B, ncp, KV_GROUPS * HEAD_DIM), BF16),
        compiler_params=_cparams(("arbitrary",)),
        name="nsa_compress",
    )(x16, xb, pe, w1, w2)


def _stack_queries(q, tq):
    parts = []
    for h in range(N_STACK):
        q_h = q[:, h * HEAD_DIM:(h + 1) * HEAD_DIM]
        z = jnp.zeros_like(q_h)
        parts.append(jnp.concatenate([q_h, z] if h // REP == 0 else [z, q_h], axis=1))
    return jnp.concatenate(parts, axis=0) * SCALE


def _query_key_offsets(tk, tq):
    key = lax.broadcasted_iota(jnp.int32, (tk, N_STACK * tq), 0)
    qry = lax.broadcasted_iota(jnp.int32, (tk, N_STACK * tq), 1) & (tq - 1)
    return qry - key


def _flash_step(s, m, l, acc, v_t):
    m_new = jnp.maximum(m, jnp.max(s, axis=0, keepdims=True))
    a = jnp.exp(m - m_new)
    p = jnp.exp(s - m_new)
    l_new = a * l + jnp.sum(p, axis=0, keepdims=True)
    return m_new, l_new, a * acc + _dot_tn(v_t, p.astype(BF16))


def _store_heads(acc_n, gates_t, gate_col, o_ref, tq):
    for pair in range(N_STACK // 2):
        g = (2 * pair) // REP
        blocks = []
        for h in (2 * pair, 2 * pair + 1):
            blk = acc_n[g * HEAD_DIM:(g + 1) * HEAD_DIM, h * tq:(h + 1) * tq]
            if gates_t is not None:
                c = h * 3 + gate_col
                blk = blk * gates_t[c:c + 1, :]
            blocks.append(blk)
        o_ref[0, :, pair * LANES:(pair + 1) * LANES] = jnp.concatenate(blocks, axis=0).T


def _head_rows(values, tq):
    return jnp.repeat(jnp.asarray(values, F32), tq).reshape(1, N_STACK * tq)


def _band_kernel(*refs, window, use_sink, gate_col, tq, tk):
    if use_sink:
        slope_ref, sink_ref, q_ref, k_ref, v_ref, o_ref = refs
        g_ref = None
    else:
        slope_ref, q_ref, k_ref, v_ref, g_ref, o_ref = refs
    i = pl.program_id(1)
    n = N_STACK * tq
    per_q = tq // tk
    q_stack = _stack_queries(q_ref[0], tq)
    rel = _query_key_offsets(tk, tq)
    slope = slope_ref[...]
    bias0 = slope * rel.astype(F32)
    if use_sink:
        m, l = sink_ref[...], jnp.ones((1, n), F32)
    else:
        m, l = jnp.full((1, n), M_INIT, F32), jnp.zeros((1, n), F32)
    acc = jnp.zeros((LANES, n), F32)
    last = (i + 1) * per_q - 1
    n_tiles = per_q + window // tk
    starts = [pl.multiple_of(jnp.maximum(last - j, 0) * tk, tk) for j in range(n_tiles)]
    raw = [_dot_nt(k_ref[0, pl.ds(starts[j], tk), :], q_stack) for j in range(n_tiles)]
    for j in range(n_tiles):
        c = j * tk - (tq - tk)
        v_t = v_ref[0, pl.ds(starts[j], tk), :]
        s = raw[j] - (bias0 + slope * float(c))
        keep = None
        if c < tk - 1:
            keep = rel >= -c
        if c + tq - 1 >= window:
            far = rel < window - c
            keep = far if keep is None else keep & far
        if keep is not None:
            s = jnp.where(keep, s, -jnp.inf)
        if j >= per_q:
            s = jnp.where(last >= j, s, -jnp.inf)
        m, l, acc = _flash_step(s, m, l, acc, v_t)
    gates_t = None if g_ref is None else jax.nn.sigmoid(g_ref[0].astype(F32)).T
    _store_heads(acc * (1.0 / l), gates_t, gate_col, o_ref, tq)


def _band_attention(proj, qname, kname, vname, window, slopes, sinks, gate_col, B, S):
    tq, tk = ATT_TQ, ATT_TK
    n = N_STACK * tq
    kern = functools.partial(_band_kernel, window=window, use_sink=sinks is not None, gate_col=gate_col, tq=tq, tk=tk)
    qb, kb, vb, gbk = SEG_OFF[qname] // 512, SEG_OFF[kname] // 128, SEG_OFF[vname] // 128, SEG_OFF["gb"] // 128
    row = pl.BlockSpec((1, n), lambda b, i: (0, 0))
    specs = [pl.BlockSpec((1, tq, 512), lambda b, i: (b, i, qb)),
             pl.BlockSpec((1, S, 128), lambda b, i: (b, 0, kb)),
             pl.BlockSpec((1, S, 128), lambda b, i: (b, 0, vb))]
    args = [proj, proj, proj]
    if sinks is not None:
        specs = [row, row] + specs
        args = [_head_rows(slopes, tq), _head_rows(sinks, tq)] + args
    else:
        specs = [row] + specs + [pl.BlockSpec((1, tq, 128), lambda b, i: (b, i, gbk))]
        args = [_head_rows(slopes, tq)] + args + [proj]
    return pl.pallas_call(
        kern,
        grid=(B, S // tq),
        in_specs=specs,
        out_specs=pl.BlockSpec((1, tq, 512), lambda b, i: (b, i, 0)),
        out_shape=jax.ShapeDtypeStruct((B, S, 512), F32),
        compiler_params=_cparams(("arbitrary", "arbitrary")),
        name="band_attn_w%d" % window,
    )(*args)


def _cmp_select_kernel(slope_ref, q_ref, kc_ref, vc_ref, g_ref, ovt_ref, o_ref, sel_ref, cnt_ref, imp_s, *, tq, n_pick):
    i = pl.program_id(1)
    ncp = kc_ref.shape[2]
    n = N_STACK * tq
    q_stack = _stack_queries(q_ref[0], tq)
    t_row = i * tq + (lax.broadcasted_iota(jnp.int32, (1, n), 1) & (tq - 1))
    last_visible = (t_row - (CMP_BLOCK - 1)) >> (CMP_STRIDE.bit_length() - 1)
    gates_t = jax.nn.sigmoid(g_ref[0].astype(F32)).T

    def attend(rows):
        n_idx = lax.broadcasted_iota(jnp.int32, (rows, n), 0)
        s = _dot_nt(kc_ref[0, 0, :rows, :], q_stack) + slope_ref[:rows, :]
        s = jnp.where(n_idx <= last_visible, s, -jnp.inf)
        m = jnp.max(s, axis=0, keepdims=True)
        m = jnp.where(m == -jnp.inf, 0.0, m)
        e = jnp.exp(s - m)
        den = jnp.sum(e, axis=0, keepdims=True)
        p = e * (1.0 / jnp.maximum(den, 1e-30))
        _store_heads(_dot_tn(vc_ref[0, 0, :rows, :], p.astype(BF16)), gates_t, 0, o_ref, tq)
        psum = jnp.concatenate(
            [sum(p[:, (g * REP + r) * tq:(g * REP + r + 1) * tq] for r in range(REP)) for g in range(KV_GROUPS)],
            axis=1)
        p_hi = psum.astype(BF16)
        p_lo = (psum - p_hi.astype(F32)).astype(BF16)
        imp_s[...] = (jnp.dot(ovt_ref[:, :rows], p_hi, preferred_element_type=F32)
                      + jnp.dot(ovt_ref[:, :rows], p_lo, preferred_element_type=F32))

    needed = (i + 1) * tq // CMP_STRIDE
    prefixes = sorted({ncp * (v + 1) // CMP_PREFIXES for v in range(CMP_PREFIXES)})
    for idx, rows in enumerate(prefixes):
        below = prefixes[idx - 1] if idx else 0
        pl.when((needed > below) & (needed <= rows))(functools.partial(attend, rows))
    imp = imp_s[...]
    w = KV_GROUPS * tq
    j_idx = lax.broadcasted_iota(jnp.int32, (LANES, w), 0)
    t_sel = i * tq + (lax.broadcasted_iota(jnp.int32, (1, w), 1) & (tq - 1))
    cur = t_sel // SEL_BLOCK
    forced = (j_idx == 0) | (j_idx == cur) | (j_idx == cur - 1)
    valid = j_idx * SEL_BLOCK <= t_sel
    imp = jnp.where(valid, jnp.where(forced, FORCE_SCORE, imp), -1.0)

    def pick(_, carry):
        imp, sel = carry
        mx = jnp.max(imp, axis=0, keepdims=True)
        first = jnp.min(jnp.where(imp == mx, j_idx, LANES), axis=0, keepdims=True)
        hit = j_idx == first
        return jnp.where(hit, -2.0, imp), jnp.where(hit, 1.0, sel)

    _, sel = lax.fori_loop(0, n_pick, pick, (imp, jnp.zeros((LANES, w), F32)))
    sel_b = sel.astype(BF16)
    for g in range(KV_GROUPS):
        sel_ref[0, g] = sel_b[:, g * tq:(g + 1) * tq]
    cnt = _dot_nt(jnp.ones((8, w), BF16), sel_b)
    cnt_ref[0, 0] = cnt[0:1, :]


def _overlap_matrix_t(ncp):
    n = np.arange(ncp)[None, :]
    j = np.arange(LANES)[:, None]
    start = n * CMP_STRIDE
    end = start + CMP_BLOCK - 1
    return ((end >= j * SEL_BLOCK) & (start < j * SEL_BLOCK + SEL_BLOCK)).astype(np.float32)


def _cmp_select(proj, kvc, B, S):
    tq = CMP_TQ
    nq = S // tq
    ncp = kvc.shape[2]
    n = N_STACK * tq
    n_pick = min(N_SELECT, S // SEL_BLOCK)
    qb, gbk = SEG_OFF["qb"] // 512, SEG_OFF["gb"] // 128
    ovt = jnp.asarray(_overlap_matrix_t(ncp), BF16)
    block_end = jnp.arange(ncp, dtype=F32).reshape(ncp, 1) * CMP_STRIDE + (CMP_BLOCK - 1)
    key_bias = block_end * _head_rows(SLOPES[HA:], tq)
    kern = functools.partial(_cmp_select_kernel, tq=tq, n_pick=n_pick)
    kv_blk = (1, 1, ncp, LANES)
    return pl.pallas_call(
        kern,
        grid=(B, nq),
        in_specs=[pl.BlockSpec((ncp, n), lambda b, i: (0, 0)),
                  pl.BlockSpec((1, tq, 512), lambda b, i: (b, i, qb)),
                  pl.BlockSpec(kv_blk, lambda b, i: (0, b, 0, 0)),
                  pl.BlockSpec(kv_blk, lambda b, i: (1, b, 0, 0)),
                  pl.BlockSpec((1, tq, 128), lambda b, i: (b, i, gbk)),
                  pl.BlockSpec((LANES, ncp), lambda b, i: (0, 0))],
        out_specs=[pl.BlockSpec((1, tq, 512), lambda b, i: (b, i, 0)),
                   pl.BlockSpec((1, KV_GROUPS, LANES, tq), lambda b, i: (b, 0, 0, i)),
                   pl.BlockSpec((1, 1, 1, LANES), lambda b, i: (b, i, 0, 0))],
        out_shape=[jax.ShapeDtypeStruct((B, S, 512), F32),
                   jax.ShapeDtypeStruct((B, KV_GROUPS, LANES, S), BF16),
                   jax.ShapeDtypeStruct((B, nq, 1, LANES), F32)],
        scratch_shapes=[pltpu.VMEM((LANES, KV_GROUPS * tq), F32)],
        compiler_params=_cparams(("arbitrary", "arbitrary")),
        name="nsa_cmp_select",
    )(key_bias, proj, kvc, kvc, proj, ovt)


def _sel_kernel(flag_ref, slope_ref, q_ref, k_ref, v_ref, sel_ref, g_ref, o_ref, m_s, l_s, acc_s, *, tq, tk, nq):
    b = pl.program_id(0)
    i = pl.program_id(1)
    nk = nq * (tq // tk)
    per_q = tq // tk
    q_stack = _stack_queries(q_ref[0], tq)
    rel = _query_key_offsets(tk, tq)
    slope = slope_ref[...]
    bias0 = slope * rel.astype(F32)
    per_tile = tk // SEL_BLOCK
    key_blk = lax.broadcasted_iota(jnp.int32, (tk, LANES), 0) // SEL_BLOCK
    blk_id = lax.broadcasted_iota(jnp.int32, (tk, LANES), 1)
    m_s[...] = jnp.full(m_s.shape, M_INIT, F32)
    l_s[...] = jnp.zeros(l_s.shape, F32)
    acc_s[...] = jnp.zeros(acc_s.shape, F32)

    def tile(kt, diagonal):
        start = pl.multiple_of(kt * tk, tk)
        k_t = k_ref[0, pl.ds(start, tk), :]
        v_t = v_ref[0, pl.ds(start, tk), :]
        expand = jnp.where(blk_id == kt * per_tile + key_blk, 1.0, 0.0).astype(BF16)
        picked = [jnp.dot(expand, sel_ref[0, g], preferred_element_type=F32) for g in range(KV_GROUPS)]
        keep = jnp.concatenate([picked[h // REP] for h in range(N_STACK)], axis=1) > 0.5
        shift = i * tq - kt * tk
        if diagonal:
            keep = keep & (rel >= -shift)
        s = _dot_nt(k_t, q_stack) - (bias0 + slope * shift.astype(F32))
        s = jnp.where(keep, s, -jnp.inf)
        m, l, acc = _flash_step(s, m_s[...], l_s[...], acc_s[...], v_t)
        m_s[...] = m
        l_s[...] = l
        acc_s[...] = acc

    for d in range(per_q):
        tile((i + 1) * per_q - 1 - d, True)

    def body(j, _):
        kt = i * per_q - 1 - j

        @pl.when(flag_ref[(b * nq + i) * nk + kt] > 0)
        def _():
            tile(kt, False)

        return 0

    lax.fori_loop(0, i * per_q, body, 0)
    gates_t = jax.nn.sigmoid(g_ref[0].astype(F32)).T
    _store_heads(acc_s[...] * (1.0 / l_s[...]), gates_t, 1, o_ref, tq)


def _selected_attention(proj, sel, flags, B, S):
    tq, tk = ATT_TQ, ATT_TK
    nq = S // tq
    n = N_STACK * tq
    qb, kb, vb, gbk = SEG_OFF["qb"] // 512, SEG_OFF["kbs"] // 128, SEG_OFF["vbs"] // 128, SEG_OFF["gb"] // 128
    kern = functools.partial(_sel_kernel, tq=tq, tk=tk, nq=nq)
    grid_spec = pltpu.PrefetchScalarGridSpec(
        num_scalar_prefetch=1,
        grid=(B, nq),
        in_specs=[pl.BlockSpec((1, n), lambda b, i, f: (0, 0)),
                  pl.BlockSpec((1, tq, 512), lambda b, i, f: (b, i, qb)),
                  pl.BlockSpec((1, S, 128), lambda b, i, f: (b, 0, kb)),
                  pl.BlockSpec((1, S, 128), lambda b, i, f: (b, 0, vb)),
                  pl.BlockSpec((1, KV_GROUPS, LANES, tq), lambda b, i, f: (b, 0, 0, i)),
                  pl.BlockSpec((1, tq, 128), lambda b, i, f: (b, i, gbk))],
        out_specs=pl.BlockSpec((1, tq, 512), lambda b, i, f: (b, i, 0)),
        scratch_shapes=[pltpu.VMEM((1, n), F32), pltpu.VMEM((1, n), F32), pltpu.VMEM((LANES, n), F32)],
    )
    return pl.pallas_call(
        kern,
        grid_spec=grid_spec,
        out_shape=jax.ShapeDtypeStruct((B, S, 512), F32),
        compiler_params=_cparams(("arbitrary", "arbitrary")),
        name="nsa_selected",
    )(flags, _head_rows(SLOPES[HA:], tq), proj, proj, proj, sel, proj)


def _selection_flags(cnt, B, S):
    nq, nk = S // ATT_TQ, S // ATT_TK
    per_tile = ATT_TK // SEL_BLOCK
    c = cnt[:, :, 0, :nk * per_tile].reshape(B, nq, ATT_TQ // CMP_TQ, nk, per_tile).sum((2, 4))
    return (c > 0).astype(jnp.int32).reshape(-1)


def _stick_kernel(*refs, t, pairs):
    o_ref = refs[3 * pairs]
    i = pl.program_id(2)
    ops = [_stick_pair_ops(refs[p], refs[pairs + p], refs[2 * pairs + p], t) for p in range(pairs)]
    tiles = [(0, i)] + [(d, jnp.maximum(i - d, 0)) for d in range(1, STICK_FIRST)]
    zs = [[logits(kt) for _, kt in tiles] for logits, _, _, _, _ in ops]
    incs = [[suffix_sums(z, d == 0) for z, (d, _) in zip(zs[p], tiles)] for p, (_, suffix_sums, _, _, _) in enumerate(ops)]
    started = []
    for p, (_, _, weights, values, step) in enumerate(ops):
        carry, acc = 0.0, 0.0
        for (d, kt), z, inc in zip(tiles, zs[p], incs[p]):
            a = weights(z, inc, carry, d == 0)
            acc = acc + values(kt, a if d == 0 else jnp.where(i >= d, a, 0.0))
            carry = carry + (inc[0:1, :] if d == 0 else jnp.where(i >= d, inc[0:1, :], 0.0))
        started.append((carry, acc, step))
    for p, (carry, acc, step) in enumerate(started):
        def cond(state):
            kt, go, _, _ = state
            return (kt >= 0) & (go > 0)

        def body(state, step=step):
            kt, _, carry, acc = state
            carry, acc = step(kt, carry, acc)
            return kt - 1, (jnp.max(carry) > -STICK_EXIT).astype(jnp.int32), carry, acc

        go0 = (jnp.max(carry) > -STICK_EXIT).astype(jnp.int32)
        _, _, _, acc = lax.while_loop(cond, body, (i - STICK_FIRST, go0, carry, acc))
        o_ref[0, :, p * LANES:(p + 1) * LANES] = jnp.concatenate([acc[:HEAD_DIM, :t], acc[HEAD_DIM:, t:]], axis=0).T


def _stick_pair_ops(q_ref, k_ref, v_ref, t):
    n = 2 * t
    lane = lax.broadcasted_iota(jnp.int32, (t, LANES), 1)
    q = q_ref[0]
    zero = jnp.zeros_like(q)
    q_stack = jnp.concatenate([jnp.where(lane < HEAD_DIM, q, zero), jnp.where(lane >= HEAD_DIM, q, zero)], axis=0) * SCALE
    key = lax.broadcasted_iota(jnp.int32, (t, n), 0)
    qry = lax.broadcasted_iota(jnp.int32, (t, n), 1) & (t - 1)
    causal = key < qry
    r_i = lax.broadcasted_iota(jnp.int32, (t, t), 0)
    c_i = lax.broadcasted_iota(jnp.int32, (t, t), 1)
    suffix = jnp.where(c_i >= r_i, 1.0, 0.0).astype(BF16)
    suffix2 = jnp.concatenate([suffix, suffix], axis=1)

    def logits(kt):
        start = pl.multiple_of(kt * t, t)
        return _dot_nt(k_ref[0, pl.ds(start, t), :], q_stack)

    def suffix_sums(z, diagonal):
        log_1mb = -(jnp.maximum(z, 0.0) + jnp.log(1.0 + jnp.exp(-jnp.abs(z))))
        if diagonal:
            log_1mb = jnp.where(causal, log_1mb, 0.0)
        hi = log_1mb.astype(BF16)
        lo = (log_1mb - hi.astype(F32)).astype(BF16)
        return jnp.dot(suffix2, jnp.concatenate([hi, lo], axis=0), preferred_element_type=F32)

    def weights(z, inc, carry, diagonal):
        a = jnp.exp(z + inc + carry)
        return jnp.where(causal, a, 0.0) if diagonal else a

    def values(kt, a):
        start = pl.multiple_of(kt * t, t)
        return _dot_tn(v_ref[0, pl.ds(start, t), :], a.astype(BF16))

    def step(kt, carry, acc):
        z = logits(kt)
        inc = suffix_sums(z, False)
        return carry + inc[0:1, :], acc + values(kt, weights(z, inc, carry, False))

    return logits, suffix_sums, weights, values, step


def _stick_attention(proj, B, S):
    t = min(STICK_T, S)
    pairs = STICK_PAIRS
    qb, kb, vb = SEG_OFF["qc"] // 128, SEG_OFF["kc"] // 128, SEG_OFF["vc"] // 128
    col = lambda base, p: (lambda b, g, i: (b, i, base + g * pairs + p))
    whole = lambda base, p: (lambda b, g, i: (b, 0, base + g * pairs + p))
    specs = ([pl.BlockSpec((1, t, 128), col(qb, p)) for p in range(pairs)]
             + [pl.BlockSpec((1, S, 128), whole(kb, p)) for p in range(pairs)]
             + [pl.BlockSpec((1, S, 128), whole(vb, p)) for p in range(pairs)])
    return pl.pallas_call(
        functools.partial(_stick_kernel, t=t, pairs=pairs),
        grid=(B, HC // 2 // pairs, S // t),
        in_specs=specs,
        out_specs=pl.BlockSpec((1, t, pairs * LANES), lambda b, g, i: (b, i, g)),
        out_shape=jax.ShapeDtypeStruct((B, S, HC * HEAD_DIM), F32),
        compiler_params=_cparams(("arbitrary", "arbitrary", "arbitrary")),
        name="stick_breaking",
    )(*([proj] * (3 * pairs)))


def _group_rms(o, eps=1e-6):
    return o * lax.rsqrt(jnp.mean(o * o, axis=-1, keepdims=True) + eps)


def _outproj_kernel(oa_ref, ob1_ref, ob2_ref, ob3_ref, oc_ref, x_ref, gain_ref, w_ref, g1_ref, lg_ref, lb_ref,
                    sc2_ref, sh2_ref, wr_ref, br_ref, x1_ref, h2_ref, route_ref):
    o_b = ob1_ref[0] + ob2_ref[0] + ob3_ref[0]
    merged = jnp.concatenate([_group_rms(oa_ref[0]), _group_rms(o_b), _group_rms(oc_ref[0])], axis=-1)
    merged = (merged * gain_ref[...]).astype(BF16)
    y = jnp.dot(merged, w_ref[...], preferred_element_type=F32)
    x1 = _layer_norm(ALPHA * x_ref[0] + (1.0 + g1_ref[0]) * y) * lg_ref[...] + lb_ref[...]
    x1_ref[0] = x1
    h2 = _layer_norm(x1) * (1.0 + sc2_ref[0]) + sh2_ref[0]
    h2_ref[0] = h2
    h_hi = h2.astype(BF16)
    h_lo = (h2 - h_hi.astype(F32)).astype(BF16)
    part = jnp.dot(h_hi, wr_ref[...], preferred_element_type=F32)
    logits = (part[:, :LANES] + part[:, LANES:]
              + jnp.dot(h_lo, wr_ref[:, :LANES], preferred_element_type=F32) + br_ref[...])
    tm = logits.shape[0]
    lane = lax.broadcasted_iota(jnp.int32, (tm, LANES), 1)
    lg = jnp.where(lane < N_GROUPS, logits, -jnp.inf)
    mg = jnp.max(lg, axis=-1, keepdims=True)
    p_group = 1.0 / jnp.sum(jnp.exp(lg - mg), axis=-1, keepdims=True)
    g_sel = jnp.min(jnp.where(lg == mg, lane, LANES), axis=-1, keepdims=True)
    e_idx = lane - N_GROUPS
    in_grp = (e_idx >= g_sel * EXPERTS_PER_GROUP) & (e_idx < (g_sel + 1) * EXPERTS_PER_GROUP)
    le = jnp.where(in_grp, logits, -jnp.inf)
    m1 = jnp.max(le, axis=-1, keepdims=True)
    i1 = jnp.min(jnp.where(le == m1, e_idx, LANES), axis=-1, keepdims=True)
    le2 = jnp.where(e_idx == i1, -jnp.inf, le)
    m2 = jnp.max(le2, axis=-1, keepdims=True)
    i2 = jnp.min(jnp.where(le2 == m2, e_idx, LANES), axis=-1, keepdims=True)
    den = jnp.sum(jnp.exp(le - m1), axis=-1, keepdims=True)
    p1 = 1.0 / den
    p2 = jnp.exp(m2 - m1) / den
    gate1 = p_group * p1 / (p1 + p2)
    gate2 = p_group * p2 / (p1 + p2)
    route = jnp.where(lane == 0, i1.astype(F32),
                      jnp.where(lane == 1, i2.astype(F32),
                                jnp.where(lane == 2, gate1, jnp.where(lane == 3, gate2, 0.0))))
    route_ref[0] = route


def _out_projection(o_a, o_b1, o_b2, o_b3, o_c, x, gain, w_out, g1, ln_g, ln_b, sc2, sh2, w_r, b_r, B, S):
    D = D_MODEL
    tm = 256
    row = lambda b, i: (b, i, 0)
    vec = lambda b, i: (0, 0)
    bvec = lambda b, i: (b, 0, 0)
    return pl.pallas_call(
        _outproj_kernel,
        grid=(B, S // tm),
        in_specs=[pl.BlockSpec((1, tm, 512), row), pl.BlockSpec((1, tm, 512), row),
                  pl.BlockSpec((1, tm, 512), row), pl.BlockSpec((1, tm, 512), row),
                  pl.BlockSpec((1, tm, 1024), row), pl.BlockSpec((1, tm, D), row),
                  pl.BlockSpec((1, D), vec), pl.BlockSpec((D, D), vec),
                  pl.BlockSpec((1, 1, D), bvec), pl.BlockSpec((1, D), vec), pl.BlockSpec((1, D), vec),
                  pl.BlockSpec((1, 1, D), bvec), pl.BlockSpec((1, 1, D), bvec),
                  pl.BlockSpec((D, 2 * LANES), vec), pl.BlockSpec((1, LANES), vec)],
        out_specs=[pl.BlockSpec((1, tm, D), row), pl.BlockSpec((1, tm, D), row),
                   pl.BlockSpec((1, tm, LANES), row)],
        out_shape=[jax.ShapeDtypeStruct((B, S, D), F32), jax.ShapeDtypeStruct((B, S, D), F32),
                   jax.ShapeDtypeStruct((B, S, LANES), F32)],
        compiler_params=_cparams(("arbitrary", "arbitrary")),
        name="out_proj_norm_route",
    )(o_a, o_b1, o_b2, o_b3, o_c, x, gain, w_out, g1, ln_g, ln_b, sc2, sh2, w_r, b_r)


ROW_UNROLL = 8


def _row_copy(src_ref, src_row, dst_ref, dst_row, sem):
    return pltpu.make_async_copy(src_ref.at[pl.ds(src_row, 1)], dst_ref.at[pl.ds(dst_row, 1)], sem)


def _dispatch_kernel(dest_ref, pend_ref, h_ref, xs_ref, zero_s, sem, zsem, *, tm, bm):
    i = pl.program_id(0)

    @pl.when(i == 0)
    def _():
        zero_s[...] = jnp.zeros(zero_s.shape, F32)

        def zero_copy(e):
            start = pl.multiple_of(jnp.maximum(pend_ref[e] - bm, 0), bm)
            return pltpu.make_async_copy(zero_s, xs_ref.at[pl.ds(start, bm)], zsem)

        def has_rows(e):
            return pend_ref[e] > (pend_ref[e - 1] if e else 0)

        for e in range(N_EXPERTS):
            pl.when(has_rows(e))(lambda e=e: zero_copy(e).start())
        for e in range(N_EXPERTS):
            pl.when(has_rows(e))(lambda e=e: zero_copy(e).wait())

        def tail_copy(blk):
            return pltpu.make_async_copy(zero_s, xs_ref.at[pl.ds(pl.multiple_of(blk * bm, bm), bm)], zsem)

        def tail_start(blk, carry):
            tail_copy(blk).start()
            return carry

        def tail_wait(blk, carry):
            tail_copy(blk).wait()
            return carry

        first_free = pend_ref[N_EXPERTS - 1] // bm
        lax.fori_loop(first_free, xs_ref.shape[0] // bm, tail_start, 0)
        lax.fori_loop(first_free, xs_ref.shape[0] // bm, tail_wait, 0)

    def body(blk, _):
        for u in range(ROW_UNROLL):
            r = blk * ROW_UNROLL + u
            for k in range(2):
                _row_copy(h_ref, r, xs_ref, dest_ref[(i * tm + r) * 2 + k], sem).start()
        return 0

    lax.fori_loop(0, tm // ROW_UNROLL, body, 0)
    for k in range(2):
        pltpu.make_async_copy(h_ref, xs_ref.at[pl.ds(0, tm)], sem).wait()


def _dispatch(dest, pad_end, h2, rows, bm):
    N, D = h2.shape
    tm = MOE_TM
    grid_spec = pltpu.PrefetchScalarGridSpec(
        num_scalar_prefetch=2,
        grid=(N // tm,),
        in_specs=[pl.BlockSpec((tm, D), lambda i, d, p: (i, 0))],
        out_specs=pl.BlockSpec(memory_space=pl.ANY),
        scratch_shapes=[pltpu.VMEM((bm, D), F32), pltpu.SemaphoreType.DMA(()), pltpu.SemaphoreType.DMA(())],
    )
    return pl.pallas_call(
        functools.partial(_dispatch_kernel, tm=tm, bm=bm),
        grid_spec=grid_spec,
        out_shape=jax.ShapeDtypeStruct((rows, D), F32),
        compiler_params=_cparams(("arbitrary",)),
        name="moe_dispatch",
    )(dest, pad_end, h2)


def _expert_kernel(be_ref, nb_ref, slot_ref, next_ref, x_ref, wg_hbm, wu_hbm, wd_hbm, y_ref,
                   fg, fu, fd, wg_s, wu_s, wd_s, sem, *, layer):
    i = pl.program_id(0)

    def fetch(e, slot):
        return [pltpu.make_async_copy(wg_hbm.at[layer, e], fg.at[slot], sem.at[slot]),
                pltpu.make_async_copy(wu_hbm.at[layer, e], fu.at[slot], sem.at[slot]),
                pltpu.make_async_copy(wd_hbm.at[layer, e], fd.at[slot], sem.at[slot])]

    @pl.when(i < nb_ref[0])
    def _():
        e = be_ref[i]
        slot = slot_ref[i]

        @pl.when(i == 0)
        def _():
            for cp in fetch(e, slot):
                cp.start()

        @pl.when((i == 0) | (e != be_ref[jnp.maximum(i - 1, 0)]))
        def _():
            for cp in fetch(e, slot):
                cp.wait()
            wg_s[...] = fg[slot].astype(BF16)
            wu_s[...] = fu[slot].astype(BF16)
            wd_s[...] = fd[slot].astype(BF16)

            @pl.when(next_ref[i] >= 0)
            def _():
                for cp in fetch(next_ref[i], 1 - slot):
                    cp.start()

        x = x_ref[...].astype(BF16)
        hg = jnp.dot(x, wg_s[...], preferred_element_type=F32)
        hu = jnp.dot(x, wu_s[...], preferred_element_type=F32)
        hid = (hg * jax.nn.sigmoid(hg)) * hu
        y_ref[...] = jnp.dot(hid.astype(BF16), wd_s[...], preferred_element_type=F32)

    @pl.when(i >= nb_ref[0])
    def _():
        y_ref[...] = jnp.zeros(y_ref.shape, F32)


def _expert_mlp(blk_expert, n_used, blk_slot, blk_next, xs, w_gate, w_up, w_down, layer, bm):
    rows, D = xs.shape
    nblk = rows // bm
    blk = lambda i, be, nb, sl, nx: (jnp.minimum(i, nb[0] - 1), 0)
    hbm = pl.BlockSpec(memory_space=pl.ANY)
    grid_spec = pltpu.PrefetchScalarGridSpec(
        num_scalar_prefetch=4,
        grid=(nblk,),
        in_specs=[pl.BlockSpec((bm, D), blk), hbm, hbm, hbm],
        out_specs=pl.BlockSpec((bm, D), lambda i, be, nb, sl, nx: (i, 0)),
        scratch_shapes=[pltpu.VMEM((2, D, D_EXPERT), F32), pltpu.VMEM((2, D, D_EXPERT), F32),
                        pltpu.VMEM((2, D_EXPERT, D), F32),
                        pltpu.VMEM((D, D_EXPERT), BF16), pltpu.VMEM((D, D_EXPERT), BF16),
                        pltpu.VMEM((D_EXPERT, D), BF16), pltpu.SemaphoreType.DMA((2,))],
    )
    return pl.pallas_call(
        functools.partial(_expert_kernel, layer=layer),
        grid_spec=grid_spec,
        out_shape=jax.ShapeDtypeStruct((rows, D), F32),
        compiler_params=_cparams(("arbitrary",)),
        name="expert_mlp",
    )(blk_expert, n_used, blk_slot, blk_next, xs, w_gate, w_up, w_down)


def _combine_kernel(dest_ref, x_ref, route_ref, g2_ref, lg_ref, lb_ref, yb_ref, o_ref, buf, sem, *, tm):
    i = pl.program_id(0)

    def body(blk, _):
        for u in range(ROW_UNROLL):
            r = blk * ROW_UNROLL + u
            for k in range(2):
                _row_copy(yb_ref, dest_ref[(i * tm + r) * 2 + k], buf.at[k], r, sem).start()
        return 0

    lax.fori_loop(0, tm // ROW_UNROLL, body, 0)
    for k in range(2):
        pltpu.make_async_copy(yb_ref.at[pl.ds(0, tm)], buf.at[k], sem).wait()
    route = route_ref[...]
    y = buf[0] * route[:, 2:3] + buf[1] * route[:, 3:4]
    o_ref[...] = _layer_norm(ALPHA * x_ref[...] + (1.0 + g2_ref[0]) * y) * lg_ref[...] + lb_ref[...]


def _combine_post_norm(dest, x1, route, g2, ln_g, ln_b, yb, S):
    N, D = x1.shape
    tm = MOE_TM
    grid_spec = pltpu.PrefetchScalarGridSpec(
        num_scalar_prefetch=1,
        grid=(N // tm,),
        in_specs=[pl.BlockSpec((tm, D), lambda i, d: (i, 0)),
                  pl.BlockSpec((tm, LANES), lambda i, d: (i, 0)),
                  pl.BlockSpec((1, 1, D), lambda i, d: (i * tm // S, 0, 0)),
                  pl.BlockSpec((1, D), lambda i, d: (0, 0)), pl.BlockSpec((1, D), lambda i, d: (0, 0)),
                  pl.BlockSpec(memory_space=pl.ANY)],
        out_specs=pl.BlockSpec((tm, D), lambda i, d: (i, 0)),
        scratch_shapes=[pltpu.VMEM((2, tm, D), F32), pltpu.SemaphoreType.DMA(())],
    )
    return pl.pallas_call(
        functools.partial(_combine_kernel, tm=tm),
        grid_spec=grid_spec,
        out_shape=jax.ShapeDtypeStruct((N, D), F32),
        compiler_params=_cparams(("arbitrary",)),
        name="moe_combine_post_norm",
    )(dest, x1, route, g2, ln_g, ln_b, yb)


MOE_BM = 256
MOE_TM = 512


def _moe(h2, x1, route, g2, ln_g, ln_b, w_gate, w_up, w_down, layer, B, S):
    D = D_MODEL
    N = B * S
    A = 2 * N
    bm = MOE_BM
    r = route.reshape(N, LANES)
    e_flat = r[:, 0:2].astype(jnp.int32).reshape(A)
    onehot = (e_flat[:, None] == jnp.arange(N_EXPERTS)[None, :]).astype(jnp.int32)
    csum = jnp.cumsum(onehot, axis=0)
    counts = csum[-1]
    rank = jnp.sum((csum - onehot) * onehot, axis=1)
    padded = (counts + bm - 1) // bm * bm
    pad_end = jnp.cumsum(padded).astype(jnp.int32)
    pad_start = pad_end - padded
    dest = (pad_start[e_flat] + rank).astype(jnp.int32)
    nblk = -(-A // bm) + N_EXPERTS
    blk_start = jnp.arange(nblk) * bm
    blk_expert = jnp.minimum(jnp.sum(blk_start[:, None] >= pad_end[None, :], axis=1), N_EXPERTS - 1).astype(jnp.int32)
    n_used = (pad_end[-1] // bm).astype(jnp.int32).reshape(1)
    xs = _dispatch(dest, pad_end, h2.reshape(N, D), nblk * bm, bm)
    first_blk = jnp.concatenate([jnp.ones((1,), bool), blk_expert[1:] != blk_expert[:-1]])
    blk_slot = ((jnp.cumsum(first_blk) - 1) & 1).astype(jnp.int32)
    after = pad_end[blk_expert] // bm
    blk_next = jnp.where(after < n_used[0], blk_expert[jnp.minimum(after, nblk - 1)], -1).astype(jnp.int32)
    yb = _expert_mlp(blk_expert, n_used, blk_slot, blk_next, xs, w_gate, w_up, w_down, layer, bm)
    out = _combine_post_norm(dest, x1.reshape(N, D), r, g2, ln_g, ln_b, yb, S)
    return out.reshape(B, S, D)


def kernel(x, c, w_ada, b_ada, w_in, sinks, cmp_pe_k, cmp_w1_k, cmp_w2_k, cmp_pe_v, cmp_w1_v, cmp_w2_v, mix_gain, w_out, ln1_g, ln1_b, w_rg, b_rg, w_re, b_re, w_gate, w_up, w_down, ln2_g, ln2_b):
    B, S, D = x.shape
    L = w_ada.shape[0]
    mod = _ada_mod(c, w_ada, b_ada)
    for l in range(L):
        sh1, sc1, g1, sh2, sc2, g2 = [mod[l, :, k * D:(k + 1) * D].reshape(B, 1, D) for k in range(6)]
        proj = _in_projection(x.reshape(B * S, D), sc1, sh1, _pack_w_in(w_in[l]), S).reshape(B, S, D_PROJ)
        o_a = _band_attention(proj, "qa", "ka", "va", WINDOW_A, SLOPES[:HA], sinks[l], 0, B, S)
        kvc = _compress(proj, jnp.stack([cmp_pe_k[l], cmp_pe_v[l]]), jnp.stack([cmp_w1_k[l], cmp_w1_v[l]]),
                        jnp.stack([cmp_w2_k[l], cmp_w2_v[l]]), B, S)
        o_cmp, sel, cnt = _cmp_select(proj, kvc, B, S)
        o_slc = _selected_attention(proj, sel, _selection_flags(cnt, B, S), B, S)
        o_win = _band_attention(proj, "qb", "kbw", "vbw", WINDOW_B, SLOPES[HA:], None, 2, B, S)
        o_c = _stick_attention(proj, B, S)
        w_r = jnp.concatenate([w_rg[l], w_re[l].transpose(1, 0, 2).reshape(D, N_EXPERTS),
                               jnp.zeros((D, LANES - N_GROUPS - N_EXPERTS), F32)], axis=1)
        b_r = jnp.concatenate([b_rg[l], b_re[l].reshape(N_EXPERTS),
                               jnp.zeros((LANES - N_GROUPS - N_EXPERTS,), F32)]).reshape(1, LANES)
        w_r_hi = w_r.astype(BF16)
        w_r_lo = (w_r - w_r_hi.astype(F32)).astype(BF16)
        x1, h2, route = _out_projection(o_a, o_cmp, o_slc, o_win, o_c, x, mix_gain[l].reshape(1, D),
                                        w_out[l].astype(BF16), g1, ln1_g[l].reshape(1, D), ln1_b[l].reshape(1, D),
                                        sc2, sh2, jnp.concatenate([w_r_hi, w_r_lo], axis=1), b_r, B, S)
        x = _moe(h2, x1, route, g2, ln2_g[l].reshape(1, D), ln2_b[l].reshape(1, D), w_gate, w_up, w_down, l, B, S)
    return x
```
